```python
import math
import jax, jax.numpy as jnp
from jax import lax
import numpy as np

D_MODEL = 1024
BATCH = 4
SEQ = 8192
DEPTH = 2

HEAD_DIM = 64
ROT_DIM = HEAD_DIM // 4
ROPE_THETA = 500000.0
Q_BLOCK = 128
EPS = 1e-6

DIFF_HEADS = 4
DIFF_VDIM = 2 * HEAD_DIM
A_Q = DIFF_HEADS * 2 * HEAD_DIM
A_K = DIFF_HEADS * 2 * HEAD_DIM
A_V = DIFF_HEADS * DIFF_VDIM
FOX_HEADS = 8
B_Q = FOX_HEADS * HEAD_DIM
B_K = FOX_HEADS * HEAD_DIM
B_V = FOX_HEADS * HEAD_DIM
B_F = FOX_HEADS
EVEN_WIDTHS = (A_Q, A_K, A_V, B_Q, B_K, B_V, B_F)
EVEN_IN = A_Q + A_K + A_V + B_Q + B_K + B_V + B_F
EVEN_MIX = A_V + B_V

DSA_HEADS = 16
C_Q = DSA_HEADS * HEAD_DIM
C_K = DSA_HEADS * HEAD_DIM
C_V = DSA_HEADS * HEAD_DIM
IDX_HEADS = 8
IDX_DIM = 64
IDX_Q = IDX_HEADS * IDX_DIM
IDX_K = IDX_DIM
IDX_W = IDX_HEADS
TOPK_MAX = 256
ODD_WIDTHS = (C_Q, C_K, C_V, IDX_Q, IDX_K, IDX_W)
ODD_IN = C_Q + C_K + C_V + IDX_Q + IDX_K + IDX_W
ODD_MIX = C_V

D_FF = 4 * D_MODEL
N_EVEN = (DEPTH + 1) // 2
N_ODD = DEPTH // 2

kernel_name = "hybrid_diff_fox_dsa_block"


def _split(x, widths):
    outs, off = [], 0
    for w in widths:
        outs.append(x[..., off:off + w])
        off += w
    return outs


def rmsnorm(x, g):
    xf = x.astype(jnp.float32)
    y = xf * lax.rsqrt(jnp.mean(xf * xf, axis=-1, keepdims=True) + EPS) * g.astype(jnp.float32)
    return y.astype(x.dtype)


def layernorm(x, g, b):
    xf = x.astype(jnp.float32)
    mu = jnp.mean(xf, axis=-1, keepdims=True)
    var = jnp.mean(jnp.square(xf - mu), axis=-1, keepdims=True)
    y = (xf - mu) * lax.rsqrt(var + EPS) * g.astype(jnp.float32) + b.astype(jnp.float32)
    return y.astype(x.dtype)


def rope_tables(seq):
    pos = jnp.arange(seq, dtype=jnp.float32)
    inv_freq = ROPE_THETA ** (-jnp.arange(0, ROT_DIM, 2, dtype=jnp.float32) / ROT_DIM)
    ang = pos[:, None] * inv_freq[None, :]
    return jnp.cos(ang)[:, None, :], jnp.sin(ang)[:, None, :]


def partial_rope(x, cos, sin):
    xf = x.astype(jnp.float32)
    half = ROT_DIM // 2
    x1, x2, xp = xf[..., :half], xf[..., half:ROT_DIM], xf[..., ROT_DIM:]
    y = jnp.concatenate([x1 * cos - x2 * sin, x2 * cos + x1 * sin, xp], axis=-1)
    return y.astype(x.dtype)


def _unblock(o, b, s):
    return jnp.moveaxis(o, 0, 1).reshape((b, s) + o.shape[3:])


def even_mixer(h, w_in, b_f, lq1, lk1, lq2, lk2, sub_g, w_out, lambda_init, cos, sin):
    b, s, _ = h.shape
    proj = h @ w_in
    a_q, a_k, a_v, f_q, f_k, f_v, f_logit = _split(proj, EVEN_WIDTHS)
    a_q = partial_rope(a_q.reshape(b, s, DIFF_HEADS * 2, HEAD_DIM), cos, sin).reshape(b, s, DIFF_HEADS, 2, HEAD_DIM)
    a_k = partial_rope(a_k.reshape(b, s, DIFF_HEADS * 2, HEAD_DIM), cos, sin).reshape(b, s, DIFF_HEADS, 2, HEAD_DIM)
    a_v = a_v.reshape(b, s, DIFF_HEADS, DIFF_VDIM)
    lam = (jnp.exp(jnp.sum(lq1.astype(jnp.float32) * lk1.astype(jnp.float32)))
           - jnp.exp(jnp.sum(lq2.astype(jnp.float32) * lk2.astype(jnp.float32)))
           + lambda_init)
    f_q = f_q.reshape(b, s, FOX_HEADS, HEAD_DIM)
    f_k = f_k.reshape(b, s, FOX_HEADS, HEAD_DIM)
    f_v = f_v.reshape(b, s, FOX_HEADS, HEAD_DIM)
    log_f = jax.nn.log_sigmoid(f_logit.astype(jnp.float32) + b_f.astype(jnp.float32))
    c = jnp.transpose(jnp.cumsum(log_f, axis=1), (0, 2, 1))

    scale = HEAD_DIM ** -0.5
    kpos = jnp.arange(s)

    def block(i):
        start = i * Q_BLOCK
        qpos = start + jnp.arange(Q_BLOCK)
        causal = kpos[None, :] <= qpos[:, None]
        qa = lax.dynamic_slice_in_dim(a_q, start, Q_BLOCK, axis=1)
        sa = jnp.einsum('bqhcd,bkhcd->bhcqk', qa, a_k).astype(jnp.float32) * scale
        pa = jax.nn.softmax(jnp.where(causal, sa, -jnp.inf), axis=-1)
        attn = pa[:, :, 0] - lam * pa[:, :, 1]
        oa = jnp.einsum('bhqk,bkhe->bqhe', attn, a_v)
        qb = lax.dynamic_slice_in_dim(f_q, start, Q_BLOCK, axis=1)
        cq = lax.dynamic_slice_in_dim(c, start, Q_BLOCK, axis=2)
        sb = jnp.einsum('bqhd,bkhd->bhqk', qb, f_k).astype(jnp.float32) * scale
        sb = sb + cq[..., None] - c[:, :, None, :]
        pb = jax.nn.softmax(jnp.where(causal, sb, -jnp.inf), axis=-1)
        ob = jnp.einsum('bhqk,bkhd->bqhd', pb, f_v)
        return oa.astype(h.dtype), ob.astype(h.dtype)

    oa, ob = lax.map(block, jnp.arange(s // Q_BLOCK))
    oa = _unblock(oa, b, s)
    ob = _unblock(ob, b, s)
    oa = (rmsnorm(oa, sub_g) * (1.0 - lambda_init)).astype(h.dtype)
    mixed = jnp.concatenate([oa.reshape(b, s, A_V), ob.reshape(b, s, B_V)], axis=-1)
    return mixed @ w_out


def odd_mixer(h, w_in, ln_g, ln_b, w_out, cos, sin):
    b, s, _ = h.shape
    proj = h @ w_in
    q, k, v, iq, ik, iw = _split(proj, ODD_WIDTHS)
    q = partial_rope(q.reshape(b, s, DSA_HEADS, HEAD_DIM), cos, sin)
    k = partial_rope(k.reshape(b, s, DSA_HEADS, HEAD_DIM), cos, sin)
    v = v.reshape(b, s, DSA_HEADS, HEAD_DIM)
    iq = partial_rope(iq.reshape(b, s, IDX_HEADS, IDX_DIM), cos, sin)
    ik = partial_rope(layernorm(ik, ln_g, ln_b)[:, :, None, :], cos, sin)[:, :, 0, :]
    iw = iw.astype(jnp.float32) * IDX_HEADS ** -0.5
    k_sel = min(TOPK_MAX, s // 4)
    scale = HEAD_DIM ** -0.5
    kpos = jnp.arange(s)
    gather = jax.vmap(lambda arr, ids: arr[ids])

    def block(i):
        start = i * Q_BLOCK
        qpos = start + jnp.arange(Q_BLOCK)
        causal = kpos[None, :] <= qpos[:, None]
        iqb = lax.dynamic_slice_in_dim(iq, start, Q_BLOCK, axis=1)
        iwb = lax.dynamic_slice_in_dim(iw, start, Q_BLOCK, axis=1)
        logits = jnp.einsum('bqhd,bkd->bqhk', iqb, ik).astype(jnp.float32) * IDX_DIM ** -0.5
        score = jnp.einsum('bqhk,bqh->bqk', jax.nn.relu(logits), iwb)
        score = jnp.where(causal[None], score, -jnp.inf)
        _, idx = lax.top_k(score, k_sel)
        valid = idx <= qpos[None, :, None]
        kg = gather(k, idx)
        vg = gather(v, idx)
        qb = lax.dynamic_slice_in_dim(q, start, Q_BLOCK, axis=1)
        sc = jnp.einsum('bqhd,bqjhd->bhqj', qb, kg).astype(jnp.float32) * scale
        p = jax.nn.softmax(jnp.where(valid[:, None], sc, -jnp.inf), axis=-1)
        o = jnp.einsum('bhqj,bqjhd->bqhd', p, vg)
        return o.astype(h.dtype)

    o = _unblock(lax.map(block, jnp.arange(s // Q_BLOCK)), b, s)
    return o.reshape(b, s, ODD_MIX) @ w_out


def sqrelu_mlp(h, w1, w2):
    return jnp.square(jax.nn.relu(h @ w1)) @ w2


def setup_inputs(seed: int = 0) -> dict:
    key = jax.random.key(seed)
    ks = jax.random.split(key, 19)
    f32 = jnp.float32
    nrm = lambda k, shp, sc: jax.random.normal(k, shp, f32) * sc
    return {
        "x": jax.random.normal(ks[0], (BATCH, SEQ, D_MODEL), f32),
        "norm_mix": 1.0 + nrm(ks[1], (DEPTH, D_MODEL), 0.02),
        "w_in_even": nrm(ks[2], (N_EVEN, D_MODEL, EVEN_IN), D_MODEL ** -0.5),
        "b_forget": jax.random.uniform(ks[3], (N_EVEN, FOX_HEADS), f32, 1.0, 4.0),
        "lambda_q1": nrm(ks[4], (N_EVEN, HEAD_DIM), 0.1),
        "lambda_k1": nrm(ks[5], (N_EVEN, HEAD_DIM), 0.1),
        "lambda_q2": nrm(ks[6], (N_EVEN, HEAD_DIM), 0.1),
        "lambda_k2": nrm(ks[7], (N_EVEN, HEAD_DIM), 0.1),
        "diff_subln_g": 1.0 + nrm(ks[8], (N_EVEN, DIFF_VDIM), 0.02),
        "w_out_even": nrm(ks[9], (N_EVEN, EVEN_MIX, D_MODEL), EVEN_MIX ** -0.5),
        "w_in_odd": nrm(ks[10], (N_ODD, D_MODEL, ODD_IN), D_MODEL ** -0.5),
        "idx_ln_g": 1.0 + nrm(ks[11], (N_ODD, IDX_DIM), 0.02),
        "idx_ln_b": nrm(ks[12], (N_ODD, IDX_DIM), 0.02),
        "w_out_odd": nrm(ks[13], (N_ODD, ODD_MIX, D_MODEL), ODD_MIX ** -0.5),
        "norm_mlp": 1.0 + nrm(ks[14], (DEPTH, D_MODEL), 0.02),
        "w_mlp_in": nrm(ks[15], (DEPTH, D_MODEL, D_FF), D_MODEL ** -0.5),
        "w_mlp_out": nrm(ks[16], (DEPTH, D_FF, D_MODEL), D_FF ** -0.5),
        "norm_final": 1.0 + nrm(ks[17], (D_MODEL,), 0.02),
    }


def reference(x, norm_mix, w_in_even, b_forget, lambda_q1, lambda_k1, lambda_q2, lambda_k2,
              diff_subln_g, w_out_even, w_in_odd, idx_ln_g, idx_ln_b, w_out_odd,
              norm_mlp, w_mlp_in, w_mlp_out, norm_final):
    cos, sin = rope_tables(x.shape[1])
    h = x
    for layer in range(DEPTH):
        j = layer // 2
        hn = rmsnorm(h, norm_mix[layer])
        if layer % 2 == 0:
            lambda_init = 0.8 - 0.6 * math.exp(-0.3 * layer)
            mix = even_mixer(hn, w_in_even[j], b_forget[j], lambda_q1[j], lambda_k1[j],
                             lambda_q2[j], lambda_k2[j], diff_subln_g[j], w_out_even[j],
                             lambda_init, cos, sin)
        else:
            mix = odd_mixer(hn, w_in_odd[j], idx_ln_g[j], idx_ln_b[j], w_out_odd[j], cos, sin)
        h = h + mix.astype(h.dtype)
        h = h + sqrelu_mlp(rmsnorm(h, norm_mlp[layer]), w_mlp_in[layer], w_mlp_out[layer]).astype(h.dtype)
    return rmsnorm(h, norm_final)
```

```python
import functools
import math

import jax
import jax.numpy as jnp
from jax import lax
from jax.experimental import pallas as pl
from jax.experimental.pallas import tpu as pltpu

F32 = jnp.float32
BF16 = jnp.bfloat16
I32 = jnp.int32

HEAD_DIM = 64
ROT_DIM = HEAD_DIM // 4
ROT_HALF = ROT_DIM // 2
ROPE_THETA = 500000.0
EPS = 1e-6
DIFF_HEADS = 4
FOX_HEADS = 8
DSA_HEADS = 16
IDX_HEADS = 8
TOPK_MAX = 256

LANES = 128
NEG = -1e30
INT_MIN = -(2 ** 31)
VMEM_LIMIT = 56 * 1024 * 1024

ROW_TILE = 512
FF_TILE = 1024
CUMSUM_TILE = 256
L0_TILE = 512
DSA_Q_TILE = 256
DSA_K_TILE = 512


def _params(sem):
    return pltpu.CompilerParams(dimension_semantics=sem, vmem_limit_bytes=VMEM_LIMIT)


def _rms(x, g):
    return x * lax.rsqrt(jnp.mean(x * x, axis=1, keepdims=True) + EPS) * g


def _dot(a, b):
    return jnp.dot(a, b, preferred_element_type=F32)


def _dot_nt(a, b):
    return lax.dot_general(a, b, (((1,), (1,)), ((), ())), preferred_element_type=F32)


def _rope_lanes(x, c, s1, s2):
    return x * c + pltpu.roll(x, ROT_HALF, 1) * s1 + pltpu.roll(x, LANES - ROT_HALF, 1) * s2


def _rope_rows(xh, cos_t, sin_t):
    x1 = xh[0:ROT_HALF]
    x2 = xh[ROT_HALF:ROT_DIM]
    return jnp.concatenate(
        [x1 * cos_t - x2 * sin_t, x2 * cos_t + x1 * sin_t, xh[ROT_DIM:]], axis=0)


def _even_proj_kernel(x_ref, g_ref, w_ref, wt_ref, c_ref, s1_ref, s2_ref, cos_ref, sin_ref,
                      aq_ref, av_ref, fq_ref, fv_ref, fl_ref, akt_ref, fkt_ref):
    hn = _rms(x_ref[...], g_ref[...]).astype(BF16)
    scale = HEAD_DIM ** -0.5
    c, s1, s2 = c_ref[...], s1_ref[...], s2_ref[...]
    for grp in range(4):
        lo = grp * LANES
        aq = _dot(hn, w_ref[:, lo:lo + LANES])
        aq_ref[:, lo:lo + LANES] = (_rope_lanes(aq, c, s1, s2) * scale).astype(BF16)
    av_ref[...] = _dot(hn, w_ref[:, 512:1024]).astype(BF16)
    fq_ref[...] = (_dot(hn, w_ref[:, 1024:1536]) * scale).astype(BF16)
    fv_ref[...] = _dot(hn, w_ref[:, 1536:2048]).astype(BF16)
    fl_ref[...] = _dot(hn, w_ref[:, 2048:2048 + LANES])
    cos_t, sin_t = cos_ref[...], sin_ref[...]
    for hd in range(2 * DIFF_HEADS):
        lo = hd * HEAD_DIM
        kt = _dot_nt(wt_ref[lo:lo + HEAD_DIM, :], hn)
        akt_ref[0, lo:lo + HEAD_DIM, :] = _rope_rows(kt, cos_t, sin_t).astype(BF16)
    fkt_ref[0] = _dot_nt(wt_ref[512:1024, :], hn).astype(BF16)


def _even_proj(h2d, g, w_in, tabs, b, s):
    n, d = h2d.shape
    tm = min(ROW_TILE, s)
    nb = s // tm
    a_q, a_k, a_v, f_q, f_k, f_v, f_l = (
        w_in[:, 0:512], w_in[:, 512:1024], w_in[:, 1024:1536], w_in[:, 1536:2048],
        w_in[:, 2048:2560], w_in[:, 2560:3072], w_in[:, 3072:3080])
    f_l = jnp.pad(f_l, ((0, 0), (0, LANES - FOX_HEADS)))
    w = jnp.concatenate([a_q, a_v, f_q, f_v, f_l], axis=1).astype(BF16)
    wt = jnp.concatenate([a_k, f_k], axis=1).T.astype(BF16)
    c, s1, s2, cos_t, sin_t = tabs
    row = lambda i: (i, 0)
    pos = lambda i: (i % nb, 0)
    post = lambda i: (0, i % nb)
    kt_map = lambda i: (i // nb, 0, i % nb)
    const = lambda i: (0, 0)
    out_shape = (
        jax.ShapeDtypeStruct((n, 512), BF16), jax.ShapeDtypeStruct((n, 512), BF16),
        jax.ShapeDtypeStruct((n, 512), BF16), jax.ShapeDtypeStruct((n, 512), BF16),
        jax.ShapeDtypeStruct((n, LANES), F32),
        jax.ShapeDtypeStruct((b, 512, s), BF16), jax.ShapeDtypeStruct((b, 512, s), BF16))
    return pl.pallas_call(
        _even_proj_kernel,
        grid=(n // tm,),
        in_specs=[
            pl.BlockSpec((tm, d), row), pl.BlockSpec((1, d), const),
            pl.BlockSpec(w.shape, const), pl.BlockSpec(wt.shape, const),
            pl.BlockSpec((tm, LANES), pos), pl.BlockSpec((tm, LANES), pos),
            pl.BlockSpec((tm, LANES), pos),
            pl.BlockSpec((ROT_HALF, tm), post), pl.BlockSpec((ROT_HALF, tm), post)],
        out_specs=(
            pl.BlockSpec((tm, 512), row), pl.BlockSpec((tm, 512), row),
            pl.BlockSpec((tm, 512), row), pl.BlockSpec((tm, 512), row),
            pl.BlockSpec((tm, LANES), row),
            pl.BlockSpec((1, 512, tm), kt_map), pl.BlockSpec((1, 512, tm), kt_map)),
        out_shape=out_shape,
        compiler_params=_params(("arbitrary",)),
        name="even_proj",
    )(h2d, g.reshape(1, d), w, wt, c, s1, s2, cos_t, sin_t)


def _fox_cumsum_kernel(fl_ref, bf_ref, c_ref, ct_ref, carry_ref):
    @pl.when(pl.program_id(1) == 0)
    def _():
        carry_ref[...] = jnp.zeros_like(carry_ref)

    z = fl_ref[0] + bf_ref[...]
    logf = jnp.minimum(z, 0.0) - jnp.log1p(jnp.exp(-jnp.abs(z)))
    t = z.shape[0]
    tri = (lax.broadcasted_iota(I32, (t, t), 0) >= lax.broadcasted_iota(I32, (t, t), 1)).astype(F32)
    cs = jnp.dot(tri, logf, precision=lax.Precision.HIGHEST,
                 preferred_element_type=F32) + carry_ref[...]
    c_ref[0] = cs
    ct_ref[0] = cs.T[0:FOX_HEADS, :]
    carry_ref[...] = cs[t - 1:t, :]


def _fox_cumsum(fl, b_f, b, s):
    tc = min(CUMSUM_TILE, s)
    bf = jnp.pad(b_f.astype(F32), (0, LANES - FOX_HEADS)).reshape(1, LANES)
    return pl.pallas_call(
        _fox_cumsum_kernel,
        grid=(b, s // tc),
        in_specs=[pl.BlockSpec((1, tc, LANES), lambda bi, i: (bi, i, 0)),
                  pl.BlockSpec((1, LANES), lambda bi, i: (0, 0))],
        out_specs=(pl.BlockSpec((1, tc, LANES), lambda bi, i: (bi, i, 0)),
                   pl.BlockSpec((1, FOX_HEADS, tc), lambda bi, i: (bi, 0, i))),
        out_shape=(jax.ShapeDtypeStruct((b, s, LANES), F32),
                   jax.ShapeDtypeStruct((b, FOX_HEADS, s), F32)),
        scratch_shapes=[pltpu.VMEM((1, LANES), F32)],
        compiler_params=_params(("arbitrary", "arbitrary")),
        name="fox_cumsum",
    )(fl.reshape(b, s, LANES), bf)


def _fill_masked_queries(q_ref, qz_ref, n_heads):
    lane = lax.broadcasted_iota(I32, (q_ref.shape[1], LANES), 1)
    for hd in range(n_heads):
        grp = hd // 2
        qg = q_ref[0, :, grp * LANES:(grp + 1) * LANES]
        keep = (lane < HEAD_DIM) if hd % 2 == 0 else (lane >= HEAD_DIM)
        qz_ref[hd] = jnp.where(keep, qg, jnp.zeros_like(qg))


def _softmax_update(s, hd, m_ref, l_ref):
    tk = s.shape[1]
    m_prev = m_ref[hd]
    m_next = jnp.maximum(m_prev, jnp.max(s, axis=1, keepdims=True))
    alpha = jnp.exp(m_prev - m_next)
    p = jnp.exp(s - pltpu.repeat(m_next, tk // LANES, 1))
    l_ref[hd] = alpha * l_ref[hd] + jnp.sum(p, axis=1, keepdims=True)
    m_ref[hd] = m_next
    return alpha, p


def _paired_heads_step(grp, s_pair, v_grp, m_ref, l_ref, acc_ref, even_lanes):
    upd = []
    for half in range(2):
        alpha, p = _softmax_update(s_pair[half], 2 * grp + half, m_ref, l_ref)
        upd.append(alpha * acc_ref[grp] + _dot(p.astype(BF16), v_grp))
    acc_ref[grp] = jnp.where(even_lanes, upd[0], upd[1])


def _paired_heads_finish(grp, l_ref, acc_ref, even_lanes):
    return acc_ref[grp] / jnp.where(even_lanes, l_ref[2 * grp], l_ref[2 * grp + 1])


def _diff_attn_kernel(lambda_init, q_ref, kt_ref, v_ref, lq1_ref, lk1_ref, lq2_ref, lk2_ref,
                      subg_ref, o_ref, qz_ref, m_ref, l_ref, acc_ref):
    i, j = pl.program_id(1), pl.program_id(2)
    tq, tk = q_ref.shape[1], kt_ref.shape[2]
    n_maps = 2 * DIFF_HEADS

    @pl.when(j == 0)
    def _():
        _fill_masked_queries(q_ref, qz_ref, n_maps)
        m_ref[...] = jnp.full(m_ref.shape, NEG, F32)
        l_ref[...] = jnp.zeros_like(l_ref)
        acc_ref[...] = jnp.zeros_like(acc_ref)

    def step(diagonal):
        if diagonal:
            causal = (lax.broadcasted_iota(I32, (tq, tk), 1) <= lax.broadcasted_iota(I32, (tq, tk), 0))
        for hd in range(DIFF_HEADS):
            kt_grp = kt_ref[0, hd * LANES:(hd + 1) * LANES, :]
            v_grp = v_ref[0, :, hd * LANES:(hd + 1) * LANES]
            for half in range(2):
                mp = 2 * hd + half
                s = _dot(qz_ref[mp], kt_grp)
                if diagonal:
                    s = jnp.where(causal, s, NEG)
                alpha, p = _softmax_update(s, mp, m_ref, l_ref)
                acc_ref[mp] = alpha * acc_ref[mp] + _dot(p.astype(BF16), v_grp)

    @pl.when(j < i)
    def _():
        step(False)

    @pl.when(j == i)
    def _():
        step(True)
        lam = (jnp.exp(jnp.sum(lq1_ref[...] * lk1_ref[...], axis=1, keepdims=True))
               - jnp.exp(jnp.sum(lq2_ref[...] * lk2_ref[...], axis=1, keepdims=True))
               + lambda_init)
        for hd in range(DIFF_HEADS):
            o = acc_ref[2 * hd] / l_ref[2 * hd] - lam * (acc_ref[2 * hd + 1] / l_ref[2 * hd + 1])
            o = _rms(o, subg_ref[...]) * (1.0 - lambda_init)
            o_ref[0, :, hd * LANES:(hd + 1) * LANES] = o.astype(BF16)


def _diff_attn(aq, akt, av, lq1, lk1, lq2, lk2, sub_g, lambda_init, b, s):
    t = min(L0_TILE, s)
    nt = s // t
    qmap = lambda bi, i, j: (bi, i, 0)
    ktmap = lambda bi, i, j: (bi, 0, jnp.minimum(j, i))
    vmap_ = lambda bi, i, j: (bi, jnp.minimum(j, i), 0)
    const = lambda bi, i, j: (0, 0)
    vec = lambda a: a.astype(F32).reshape(1, -1)
    return pl.pallas_call(
        functools.partial(_diff_attn_kernel, lambda_init),
        grid=(b, nt, nt),
        in_specs=[pl.BlockSpec((1, t, 512), qmap), pl.BlockSpec((1, 512, t), ktmap),
                  pl.BlockSpec((1, t, 512), vmap_),
                  pl.BlockSpec((1, HEAD_DIM), const), pl.BlockSpec((1, HEAD_DIM), const),
                  pl.BlockSpec((1, HEAD_DIM), const), pl.BlockSpec((1, HEAD_DIM), const),
                  pl.BlockSpec((1, LANES), const)],
        out_specs=pl.BlockSpec((1, t, 512), qmap),
        out_shape=jax.ShapeDtypeStruct((b, s, 512), BF16),
        scratch_shapes=[pltpu.VMEM((2 * DIFF_HEADS, t, LANES), BF16),
                        pltpu.VMEM((2 * DIFF_HEADS, t, LANES), F32),
                        pltpu.VMEM((2 * DIFF_HEADS, t, LANES), F32),
                        pltpu.VMEM((2 * DIFF_HEADS, t, LANES), F32)],
        compiler_params=_params(("arbitrary", "arbitrary", "arbitrary")),
        name="diff_attn",
    )(aq.reshape(b, s, 512), akt, av.reshape(b, s, 512), vec(lq1), vec(lk1), vec(lq2), vec(lk2),
      vec(sub_g))


def _fox_attn_kernel(q_ref, kt_ref, v_ref, c_ref, ct_ref, o_ref, qz_ref, cq_ref, m_ref, l_ref,
                     acc_ref):
    i, j = pl.program_id(1), pl.program_id(2)
    tq, tk = q_ref.shape[1], kt_ref.shape[2]
    even_lanes = lax.broadcasted_iota(I32, (tq, LANES), 1) < HEAD_DIM

    @pl.when(j == 0)
    def _():
        _fill_masked_queries(q_ref, qz_ref, FOX_HEADS)
        cq = c_ref[0]
        for hd in range(FOX_HEADS):
            cq_ref[hd] = jnp.broadcast_to(cq[:, hd:hd + 1], (tq, LANES))
        m_ref[...] = jnp.full(m_ref.shape, NEG, F32)
        l_ref[...] = jnp.zeros_like(l_ref)
        acc_ref[...] = jnp.zeros_like(acc_ref)

    def step(diagonal):
        if diagonal:
            causal = (lax.broadcasted_iota(I32, (tq, tk), 1) <= lax.broadcasted_iota(I32, (tq, tk), 0))
        ck = ct_ref[0]
        for grp in range(FOX_HEADS // 2):
            kt_grp = kt_ref[0, grp * LANES:(grp + 1) * LANES, :]
            v_grp = v_ref[0, :, grp * LANES:(grp + 1) * LANES]
            s_pair = []
            for half in range(2):
                hd = 2 * grp + half
                s = _dot(qz_ref[hd], kt_grp)
                s = s + (pltpu.repeat(cq_ref[hd], tk // LANES, 1) - ck[hd:hd + 1, :])
                if diagonal:
                    s = jnp.where(causal, s, NEG)
                s_pair.append(s)
            _paired_heads_step(grp, s_pair, v_grp, m_ref, l_ref, acc_ref, even_lanes)

    @pl.when(j < i)
    def _():
        step(False)

    @pl.when(j == i)
    def _():
        step(True)
        for grp in range(FOX_HEADS // 2):
            o = _paired_heads_finish(grp, l_ref, acc_ref, even_lanes)
            o_ref[0, :, grp * LANES:(grp + 1) * LANES] = o.astype(BF16)


def _fox_attn(fq, fkt, fv, c, ct, b, s):
    t = min(L0_TILE, s)
    nt = s // t
    qmap = lambda bi, i, j: (bi, i, 0)
    ktmap = lambda bi, i, j: (bi, 0, jnp.minimum(j, i))
    vmap_ = lambda bi, i, j: (bi, jnp.minimum(j, i), 0)
    return pl.pallas_call(
        _fox_attn_kernel,
        grid=(b, nt, nt),
        in_specs=[pl.BlockSpec((1, t, 512), qmap), pl.BlockSpec((1, 512, t), ktmap),
                  pl.BlockSpec((1, t, 512), vmap_), pl.BlockSpec((1, t, LANES), qmap),
                  pl.BlockSpec((1, FOX_HEADS, t), ktmap)],
        out_specs=pl.BlockSpec((1, t, 512), qmap),
        out_shape=jax.ShapeDtypeStruct((b, s, 512), BF16),
        scratch_shapes=[pltpu.VMEM((FOX_HEADS, t, LANES), BF16),
                        pltpu.VMEM((FOX_HEADS, t, LANES), F32),
                        pltpu.VMEM((FOX_HEADS, t, LANES), F32),
                        pltpu.VMEM((FOX_HEADS, t, LANES), F32),
                        pltpu.VMEM((FOX_HEADS // 2, t, LANES), F32)],
        compiler_params=_params(("arbitrary", "arbitrary", "arbitrary")),
        name="fox_attn",
    )(fq.reshape(b, s, 512), fkt, fv.reshape(b, s, 512), c, ct)


def _mlp_kernel(n_mix, final, *refs):
    h_ref = refs[0]
    mix_refs = refs[1:1 + n_mix]
    wo_refs = refs[1 + n_mix:1 + 2 * n_mix]
    g_ref, w1_ref, w2_ref, gf_ref, out_ref, h1_ref, hn_ref, acc_ref = refs[1 + 2 * n_mix:]
    f = pl.program_id(1)

    @pl.when(f == 0)
    def _():
        h1 = h_ref[...]
        for mix_ref, wo_ref in zip(mix_refs, wo_refs):
            h1 = h1 + _dot(mix_ref[...], wo_ref[...])
        h1_ref[...] = h1
        hn_ref[...] = _rms(h1, g_ref[...]).astype(BF16)
        acc_ref[...] = jnp.zeros_like(acc_ref)

    u = _dot(hn_ref[...], w1_ref[...])
    a = jnp.square(jnp.maximum(u, 0.0)).astype(BF16)
    acc_ref[...] += _dot(a, w2_ref[...])

    @pl.when(f == pl.num_programs(1) - 1)
    def _():
        y = h1_ref[...] + acc_ref[...]
        if final:
            y = _rms(y, gf_ref[...])
        out_ref[...] = y


def _outproj_mlp(h2d, mixes, w_outs, g, w1, w2, g_final, final, s):
    n, d = h2d.shape
    dff = w1.shape[1]
    tm = min(ROW_TILE, s)
    tf = min(FF_TILE, dff)
    row = lambda i, f: (i, 0)
    const = lambda i, f: (0, 0)
    in_specs = [pl.BlockSpec((tm, d), row)]
    in_specs += [pl.BlockSpec((tm, m.shape[1]), row) for m in mixes]
    in_specs += [pl.BlockSpec(w.shape, const) for w in w_outs]
    in_specs += [pl.BlockSpec((1, d), const),
                 pl.BlockSpec((d, tf), lambda i, f: (0, f)),
                 pl.BlockSpec((tf, d), lambda i, f: (f, 0)),
                 pl.BlockSpec((1, d), const)]
    return pl.pallas_call(
        functools.partial(_mlp_kernel, len(mixes), final),
        grid=(n // tm, dff // tf),
        in_specs=in_specs,
        out_specs=pl.BlockSpec((tm, d), row),
        out_shape=jax.ShapeDtypeStruct((n, d), F32),
        scratch_shapes=[pltpu.VMEM((tm, d), F32), pltpu.VMEM((tm, d), BF16),
                        pltpu.VMEM((tm, d), F32)],
        compiler_params=_params(("arbitrary", "arbitrary")),
        name="outproj_mlp",
    )(h2d, *mixes, *[w.astype(BF16) for w in w_outs], g.reshape(1, d), w1.astype(BF16),
      w2.astype(BF16), g_final.reshape(1, d))


def _odd_proj_kernel(x_ref, g_ref, w_ref, wt_ref, lng_ref, lnb_ref, c_ref, s1_ref, s2_ref,
                     cos_ref, sin_ref, q_ref, v_ref, iq_ref, iw_ref, kt_ref, ikt_ref):
    hn = _rms(x_ref[...], g_ref[...]).astype(BF16)
    c, s1, s2 = c_ref[...], s1_ref[...], s2_ref[...]
    for grp in range(DSA_HEADS // 2):
        lo = grp * LANES
        q = _dot(hn, w_ref[:, lo:lo + LANES])
        q_ref[:, lo:lo + LANES] = (_rope_lanes(q, c, s1, s2) * HEAD_DIM ** -0.5).astype(BF16)
    v_ref[...] = _dot(hn, w_ref[:, 1024:2048]).astype(BF16)
    for grp in range(IDX_HEADS // 2):
        lo = grp * LANES
        iq = _dot(hn, w_ref[:, 2048 + lo:2048 + lo + LANES])
        iq_ref[:, lo:lo + LANES] = (_rope_lanes(iq, c, s1, s2) * HEAD_DIM ** -0.5).astype(BF16)
    iw_ref[...] = _dot(hn, w_ref[:, 2560:2560 + LANES]) * IDX_HEADS ** -0.5
    cos_t, sin_t = cos_ref[...], sin_ref[...]
    for hd in range(DSA_HEADS):
        lo = hd * HEAD_DIM
        kt = _dot_nt(wt_ref[lo:lo + HEAD_DIM, :], hn)
        kt_ref[0, lo:lo + HEAD_DIM, :] = _rope_rows(kt, cos_t, sin_t).astype(BF16)
    ik = _dot_nt(wt_ref[1024:1024 + HEAD_DIM, :], hn)
    mu = jnp.mean(ik, axis=0, keepdims=True)
    var = jnp.mean(jnp.square(ik - mu), axis=0, keepdims=True)
    ik = (ik - mu) * lax.rsqrt(var + EPS) * lng_ref[...] + lnb_ref[...]
    ik = _rope_rows(ik, cos_t, sin_t).astype(BF16)
    ikt_ref[0, 0] = jnp.concatenate([ik, ik], axis=0)


def _odd_proj(h2d, g, w_in, ln_g, ln_b, tabs, b, s):
    n, d = h2d.shape
    tm = min(ROW_TILE, s, DSA_K_TILE)
    nb = s // tm
    wq, wk, wv, wiq, wik, wiw = (
        w_in[:, 0:1024], w_in[:, 1024:2048], w_in[:, 2048:3072], w_in[:, 3072:3584],
        w_in[:, 3584:3648], w_in[:, 3648:3656])
    wiw = jnp.pad(wiw, ((0, 0), (0, LANES - IDX_HEADS)))
    w = jnp.concatenate([wq, wv, wiq, wiw], axis=1).astype(BF16)
    wt = jnp.concatenate([wk, wik], axis=1).T.astype(BF16)
    c, s1, s2, cos_t, sin_t = tabs
    row = lambda i: (i, 0)
    pos = lambda i: (i % nb, 0)
    post = lambda i: (0, i % nb)
    const = lambda i: (0, 0)
    out_shape = (
        jax.ShapeDtypeStruct((n, 1024), BF16), jax.ShapeDtypeStruct((n, 1024), BF16),
        jax.ShapeDtypeStruct((n, 512), BF16), jax.ShapeDtypeStruct((n, LANES), F32),
        jax.ShapeDtypeStruct((b, 1024, s), BF16),
        jax.ShapeDtypeStruct((b, nb, LANES, tm), BF16))
    return pl.pallas_call(
        _odd_proj_kernel,
        grid=(n // tm,),
        in_specs=[
            pl.BlockSpec((tm, d), row), pl.BlockSpec((1, d), const),
            pl.BlockSpec(w.shape, const), pl.BlockSpec(wt.shape, const),
            pl.BlockSpec((HEAD_DIM, 1), const), pl.BlockSpec((HEAD_DIM, 1), const),
            pl.BlockSpec((tm, LANES), pos), pl.BlockSpec((tm, LANES), pos),
            pl.BlockSpec((tm, LANES), pos),
            pl.BlockSpec((ROT_HALF, tm), post), pl.BlockSpec((ROT_HALF, tm), post)],
        out_specs=(
            pl.BlockSpec((tm, 1024), row), pl.BlockSpec((tm, 1024), row),
            pl.BlockSpec((tm, 512), row), pl.BlockSpec((tm, LANES), row),
            pl.BlockSpec((1, 1024, tm), lambda i: (i // nb, 0, i % nb)),
            pl.BlockSpec((1, 1, LANES, tm), lambda i: (i // nb, i % nb, 0, 0))),
        out_shape=out_shape,
        compiler_params=_params(("arbitrary",)),
        name="odd_proj",
    )(h2d, g.reshape(1, d), w, wt, ln_g.astype(F32).reshape(HEAD_DIM, 1),
      ln_b.astype(F32).reshape(HEAD_DIM, 1), c, s1, s2, cos_t, sin_t)


def _dsa_attn_kernel(k_sel, q_ref, kt_ref, v_ref, iq_ref, ikt_ref, iw_ref, o_ref,
                     keys_ref, thr_ref, need_ref, tie_ref, flag_ref, bias_ref,
                     qz_ref, iqz_ref, wrep_ref, m_ref, l_ref, acc_ref):
    i, j = pl.program_id(1), pl.program_id(2)
    tq, tk = q_ref.shape[1], kt_ref.shape[2]
    n_sub = tk // LANES
    j_last = ((i + 1) * tq - 1) // tk
    even_lanes = lax.broadcasted_iota(I32, (tq, LANES), 1) < HEAD_DIM

    def count_ge(cand):
        def body(c, part):
            kc = keys_ref[c]
            for sub in range(n_sub):
                part = part + jnp.where(kc[:, sub * LANES:(sub + 1) * LANES] >= cand, 1, 0)
            return part
        part = lax.fori_loop(0, j_last + 1, body, jnp.zeros((tq, LANES), I32))
        return jnp.broadcast_to(jnp.sum(part, axis=1, keepdims=True), (tq, LANES))

    @pl.when(j == 0)
    def _():
        _fill_masked_queries(q_ref, qz_ref, DSA_HEADS)
        _fill_masked_queries(iq_ref, iqz_ref, IDX_HEADS)
        iw = iw_ref[0]
        for hd in range(IDX_HEADS):
            wrep_ref[hd] = jnp.broadcast_to(iw[:, hd:hd + 1], (tq, LANES))
        m_ref[...] = jnp.full(m_ref.shape, NEG, F32)
        l_ref[...] = jnp.zeros_like(l_ref)
        acc_ref[...] = jnp.zeros_like(acc_ref)
        tie_ref[...] = jnp.zeros_like(tie_ref)
        qpos = i * tq + lax.broadcasted_iota(I32, (tq, tk), 0)

        def score_tile(c, carry):
            ik = ikt_ref[0, c]
            score = jnp.zeros((tq, tk), F32)
            for hd in range(IDX_HEADS):
                lg = jnp.maximum(_dot(iqz_ref[hd], ik), 0.0)
                score = score + lg * pltpu.repeat(wrep_ref[hd], n_sub, 1)
            score = jnp.where(score == 0.0, 0.0, score)
            bits = lax.bitcast_convert_type(score, I32)
            key = bits ^ ((bits >> 31) & 0x7FFFFFFF)
            kpos = c * tk + lax.broadcasted_iota(I32, (tq, tk), 1)
            keys_ref[c] = jnp.where(kpos <= qpos, key, INT_MIN)
            return carry
        lax.fori_loop(0, j_last + 1, score_tile, 0)

        def bit_step(step, thr):
            cand = thr + lax.shift_left(jnp.int32(1), 31 - step)
            return jnp.where(count_ge(cand) >= k_sel, cand, thr)
        thr = lax.fori_loop(0, 32, bit_step, jnp.full((tq, LANES), INT_MIN, I32))
        n_ge = count_ge(thr)
        n_gt = count_ge(thr + 1)
        real = thr > INT_MIN
        thr_ref[...] = thr
        need_ref[...] = jnp.where(real, k_sel - n_gt, 0).astype(F32)
        extra = jnp.where(real & (n_ge > k_sel), 1, 0)
        flag_ref[0] = jnp.max(extra)

    @pl.when(j <= j_last)
    def _():
        thr = thr_ref[...]
        kc = keys_ref[j]

        @pl.when(flag_ref[0] == 0)
        def _():
            thr_eff = jnp.maximum(thr, INT_MIN + 1)
            for sub in range(n_sub):
                sel = kc[:, sub * LANES:(sub + 1) * LANES] >= thr_eff
                bias_ref[:, sub * LANES:(sub + 1) * LANES] = jnp.where(sel, 0.0, NEG)

        @pl.when(flag_ref[0] != 0)
        def _():
            thr_k = pltpu.repeat(thr, n_sub, 1)
            gt = kc > thr_k
            eq = (kc == thr_k) & (thr_k > INT_MIN)
            before = (lax.broadcasted_iota(I32, (tk, tk), 0) < lax.broadcasted_iota(I32, (tk, tk), 1))
            eq_f = jnp.where(eq, 1.0, 0.0)
            rank = _dot(eq_f.astype(BF16), jnp.where(before, 1.0, 0.0).astype(BF16))
            rank = rank + pltpu.repeat(tie_ref[...], n_sub, 1)
            sel = gt | (eq & (rank < pltpu.repeat(need_ref[...], n_sub, 1)))
            bias_ref[...] = jnp.where(sel, 0.0, NEG)
            tie_ref[...] = tie_ref[...] + jnp.sum(eq_f, axis=1, keepdims=True)

        bias = bias_ref[...]
        for grp in range(DSA_HEADS // 2):
            kt_grp = kt_ref[0, grp * LANES:(grp + 1) * LANES, :]
            v_grp = v_ref[0, :, grp * LANES:(grp + 1) * LANES]
            s_pair = [_dot(qz_ref[2 * grp + half], kt_grp) + bias for half in range(2)]
            _paired_heads_step(grp, s_pair, v_grp, m_ref, l_ref, acc_ref, even_lanes)

    @pl.when(j == j_last)
    def _():
        for grp in range(DSA_HEADS // 2):
            o = _paired_heads_finish(grp, l_ref, acc_ref, even_lanes)
            o_ref[0, :, grp * LANES:(grp + 1) * LANES] = o.astype(BF16)


def _dsa_attn(q, kt, v, iq, ikt, iw, b, s):
    tq = min(DSA_Q_TILE, s)
    tk = ikt.shape[3]
    nq, nk = s // tq, s // tk
    k_sel = min(TOPK_MAX, s // 4)
    assert tk >= k_sel and tk % tq == 0
    last = lambda i: ((i + 1) * tq - 1) // tk
    qmap = lambda bi, i, j: (bi, i, 0)
    ktmap = lambda bi, i, j: (bi, 0, jnp.minimum(j, last(i)))
    vmap_ = lambda bi, i, j: (bi, jnp.minimum(j, last(i)), 0)
    return pl.pallas_call(
        functools.partial(_dsa_attn_kernel, k_sel),
        grid=(b, nq, nk),
        in_specs=[pl.BlockSpec((1, tq, 1024), qmap), pl.BlockSpec((1, 1024, tk), ktmap),
                  pl.BlockSpec((1, tk, 1024), vmap_), pl.BlockSpec((1, tq, 512), qmap),
                  pl.BlockSpec((1, nk, LANES, tk), lambda bi, i, j: (bi, 0, 0, 0)),
                  pl.BlockSpec((1, tq, LANES), qmap)],
        out_specs=pl.BlockSpec((1, tq, 1024), qmap),
        out_shape=jax.ShapeDtypeStruct((b, s, 1024), BF16),
        scratch_shapes=[pltpu.VMEM((nk, tq, tk), I32),
                        pltpu.VMEM((tq, LANES), I32),
                        pltpu.VMEM((tq, LANES), F32),
                        pltpu.VMEM((tq, LANES), F32),
                        pltpu.SMEM((1,), I32),
                        pltpu.VMEM((tq, tk), F32),
                        pltpu.VMEM((DSA_HEADS, tq, LANES), BF16),
                        pltpu.VMEM((IDX_HEADS, tq, LANES), BF16),
                        pltpu.VMEM((IDX_HEADS, tq, LANES), F32),
                        pltpu.VMEM((DSA_HEADS, tq, LANES), F32),
                        pltpu.VMEM((DSA_HEADS, tq, LANES), F32),
                        pltpu.VMEM((DSA_HEADS // 2, tq, LANES), F32)],
        compiler_params=_params(("arbitrary", "arbitrary", "arbitrary")),
        name="dsa_attn",
    )(q.reshape(b, s, 1024), kt, v.reshape(b, s, 1024), iq.reshape(b, s, 512), ikt,
      iw.reshape(b, s, LANES))


def _rope_tables(s):
    pos = jnp.arange(s, dtype=F32)
    inv_freq = ROPE_THETA ** (-jnp.arange(0, ROT_DIM, 2, dtype=F32) / ROT_DIM)
    ang = pos[:, None] * inv_freq[None, :]
    cos, sin = jnp.cos(ang), jnp.sin(ang)
    zeros = jnp.zeros((s, HEAD_DIM - ROT_DIM), F32)
    zero8 = jnp.zeros((s, ROT_HALF), F32)
    c = jnp.concatenate([cos, cos, zeros + 1.0], axis=1)
    s1 = jnp.concatenate([zero8, sin, zeros], axis=1)
    s2 = jnp.concatenate([-sin, zero8, zeros], axis=1)
    two = lambda a: jnp.concatenate([a, a], axis=1)
    return two(c), two(s1), two(s2), cos.T, sin.T


def kernel(x, norm_mix, w_in_even, b_forget, lambda_q1, lambda_k1, lambda_q2, lambda_k2,
           diff_subln_g, w_out_even, w_in_odd, idx_ln_g, idx_ln_b, w_out_odd, norm_mlp,
           w_mlp_in, w_mlp_out, norm_final):
    b, s, d = x.shape
    tabs = _rope_tables(s)
    h = x.reshape(b * s, d)

    lambda_init = 0.8 - 0.6 * math.exp(-0.3 * 0)
    aq, av, fq, fv, fl, akt, fkt = _even_proj(h, norm_mix[0], w_in_even[0], tabs, b, s)
    c, ct = _fox_cumsum(fl, b_forget[0], b, s)
    oa = _diff_attn(aq, akt, av, lambda_q1[0], lambda_k1[0], lambda_q2[0], lambda_k2[0],
                    diff_subln_g[0], lambda_init, b, s)
    ob = _fox_attn(fq, fkt, fv, c, ct, b, s)
    h = _outproj_mlp(h, [oa.reshape(b * s, 512), ob.reshape(b * s, 512)],
                     [w_out_even[0][0:512], w_out_even[0][512:1024]], norm_mlp[0],
                     w_mlp_in[0], w_mlp_out[0], norm_final, False, s)

    q, v, iq, iw, kt, ikt = _odd_proj(h, norm_mix[1], w_in_odd[0], idx_ln_g[0], idx_ln_b[0],
                                      tabs, b, s)
    o = _dsa_attn(q, kt, v, iq, ikt, iw, b, s)
    h = _outproj_mlp(h, [o.reshape(b * s, 1024)], [w_out_odd[0]], norm_mlp[1],
                     w_mlp_in[1], w_mlp_out[1], norm_final, True, s)
    return h.reshape(b, s, d)
```

```python
import functools
import math

import jax
import jax.numpy as jnp
from jax import lax
from jax.experimental import pallas as pl
from jax.experimental.pallas import tpu as pltpu

F32 = jnp.float32
BF16 = jnp.bfloat16
I32 = jnp.int32

HEAD_DIM = 64
ROT_DIM = HEAD_DIM // 4
ROT_HALF = ROT_DIM // 2
ROPE_THETA = 500000.0
EPS = 1e-6
DIFF_HEADS = 4
FOX_HEADS = 8
DSA_HEADS = 16
IDX_HEADS = 8
TOPK_MAX = 256

LANES = 128
SUBLANES = 8
BF16_ROWS = 16
NEG = -1e30
INT_MIN = -(2 ** 31)
LOG2E = math.log2(math.e)
Q_SCALE = HEAD_DIM ** -0.5 * LOG2E
VMEM_LIMIT = 56 * 1024 * 1024

ROW_TILE = 512
FF_TILE = 1024
CUMSUM_TILE = 256
L0_TILE = 512
DSA_TILE = 512


def _params(sem):
    return pltpu.CompilerParams(dimension_semantics=sem, vmem_limit_bytes=VMEM_LIMIT)


def _rms(x, g):
    return x * lax.rsqrt(jnp.mean(x * x, axis=1, keepdims=True) + EPS) * g


def _dot(a, b):
    return jnp.dot(a, b, preferred_element_type=F32)


def _dot_nt(a, b):
    return lax.dot_general(a, b, (((1,), (1,)), ((), ())), preferred_element_type=F32)


def _rope_lanes(x, c, s1, s2):
    return x * c + pltpu.roll(x, ROT_HALF, 1) * s1 + pltpu.roll(x, LANES - ROT_HALF, 1) * s2


def _rope_rows(xh, cos_t, sin_t):
    x1 = xh[0:ROT_HALF]
    x2 = xh[ROT_HALF:ROT_DIM]
    return jnp.concatenate(
        [x1 * cos_t - x2 * sin_t, x2 * cos_t + x1 * sin_t, xh[ROT_DIM:]], axis=0)


def _even_proj_kernel(x_ref, g_ref, w_ref, wt_ref, c_ref, s1_ref, s2_ref, cos_ref, sin_ref,
                      ak_ref, fk_ref, fl_ref, aqt_ref, avt_ref, fqt_ref, fvt_ref):
    hn = _rms(x_ref[...], g_ref[...]).astype(BF16)
    c, s1, s2 = c_ref[...], s1_ref[...], s2_ref[...]
    for grp in range(DIFF_HEADS):
        lo = grp * LANES
        ak = _dot(hn, w_ref[:, lo:lo + LANES])
        ak_ref[:, lo:lo + LANES] = _rope_lanes(ak, c, s1, s2).astype(BF16)
    fk_ref[...] = _dot(hn, w_ref[:, 512:1024]).astype(BF16)
    fl_ref[...] = _dot(hn, w_ref[:, 1024:1024 + LANES])
    cos_t, sin_t = cos_ref[...], sin_ref[...]
    for hd in range(2 * DIFF_HEADS):
        lo = hd * HEAD_DIM
        qt = _dot_nt(wt_ref[lo:lo + HEAD_DIM, :], hn)
        aqt_ref[0, lo:lo + HEAD_DIM, :] = (_rope_rows(qt, cos_t, sin_t) * Q_SCALE).astype(BF16)
    avt_ref[0] = _dot_nt(wt_ref[512:1024, :], hn).astype(BF16)
    fqt_ref[0] = (_dot_nt(wt_ref[1024:1536, :], hn) * Q_SCALE).astype(BF16)
    fvt_ref[0] = _dot_nt(wt_ref[1536:2048, :], hn).astype(BF16)


def _even_proj(h2d, g, w_in, tabs, b, s):
    n, d = h2d.shape
    tm = min(ROW_TILE, s)
    nb = s // tm
    a_q, a_k, a_v, f_q, f_k, f_v, f_l = (
        w_in[:, 0:512], w_in[:, 512:1024], w_in[:, 1024:1536], w_in[:, 1536:2048],
        w_in[:, 2048:2560], w_in[:, 2560:3072], w_in[:, 3072:3080])
    f_l = jnp.pad(f_l, ((0, 0), (0, LANES - FOX_HEADS)))
    w = jnp.concatenate([a_k, f_k, f_l], axis=1).astype(BF16)
    wt = jnp.concatenate([a_q, a_v, f_q, f_v], axis=1).T.astype(BF16)
    c, s1, s2, cos_t, sin_t = tabs
    row = lambda i: (i, 0)
    pos = lambda i: (i % nb, 0)
    post = lambda i: (0, i % nb)
    fmaj = lambda i: (i // nb, 0, i % nb)
    const = lambda i: (0, 0)
    fm_shape = jax.ShapeDtypeStruct((b, 512, s), BF16)
    fm_spec = pl.BlockSpec((1, 512, tm), fmaj)
    return pl.pallas_call(
        _even_proj_kernel,
        grid=(n // tm,),
        in_specs=[
            pl.BlockSpec((tm, d), row), pl.BlockSpec((1, d), const),
            pl.BlockSpec(w.shape, const), pl.BlockSpec(wt.shape, const),
            pl.BlockSpec((tm, LANES), pos), pl.BlockSpec((tm, LANES), pos),
            pl.BlockSpec((tm, LANES), pos),
            pl.BlockSpec((ROT_HALF, tm), post), pl.BlockSpec((ROT_HALF, tm), post)],
        out_specs=(
            pl.BlockSpec((tm, 512), row), pl.BlockSpec((tm, 512), row),
            pl.BlockSpec((tm, LANES), row), fm_spec, fm_spec, fm_spec, fm_spec),
        out_shape=(
            jax.ShapeDtypeStruct((n, 512), BF16), jax.ShapeDtypeStruct((n, 512), BF16),
            jax.ShapeDtypeStruct((n, LANES), F32), fm_shape, fm_shape, fm_shape, fm_shape),
        compiler_params=_params(("arbitrary",)),
        name="even_proj",
    )(h2d, g.reshape(1, d), w, wt, c, s1, s2, cos_t, sin_t)


def _fox_cumsum_kernel(fl_ref, bf_ref, c_ref, carry_ref):
    @pl.when(pl.program_id(1) == 0)
    def _():
        carry_ref[...] = jnp.zeros_like(carry_ref)

    z = fl_ref[0] + bf_ref[...]
    logf = jnp.minimum(z, 0.0) - jnp.log1p(jnp.exp(-jnp.abs(z)))
    t = z.shape[0]
    tri = (lax.broadcasted_iota(I32, (t, t), 0) >= lax.broadcasted_iota(I32, (t, t), 1)).astype(F32)
    cs = jnp.dot(tri, logf, precision=lax.Precision.HIGHEST,
                 preferred_element_type=F32) + carry_ref[...]
    c_ref[0] = cs
    carry_ref[...] = cs[t - 1:t, :]


def _fox_cumsum(fl, b_f, b, s):
    tc = min(CUMSUM_TILE, s)
    bf = jnp.pad(b_f.astype(F32), (0, LANES - FOX_HEADS)).reshape(1, LANES)
    return pl.pallas_call(
        _fox_cumsum_kernel,
        grid=(b, s // tc),
        in_specs=[pl.BlockSpec((1, tc, LANES), lambda bi, i: (bi, i, 0)),
                  pl.BlockSpec((1, LANES), lambda bi, i: (0, 0))],
        out_specs=pl.BlockSpec((1, tc, LANES), lambda bi, i: (bi, i, 0)),
        out_shape=jax.ShapeDtypeStruct((b, s, LANES), F32),
        scratch_shapes=[pltpu.VMEM((1, LANES), F32)],
        compiler_params=_params(("arbitrary", "arbitrary")),
        name="fox_cumsum",
    )(fl.reshape(b, s, LANES), bf)


def _fill_masked_queries(qt_ref, qz_ref, n_heads):
    row = lax.broadcasted_iota(I32, (LANES, qt_ref.shape[2]), 0)
    for hd in range(n_heads):
        grp = hd // 2
        qg = qt_ref[0, grp * LANES:(grp + 1) * LANES, :]
        keep = (row < HEAD_DIM) if hd % 2 == 0 else (row >= HEAD_DIM)
        qz_ref[hd] = jnp.where(keep, qg, jnp.zeros_like(qg))


def _init_state(m_ref, l_ref, acc_ref):
    m_ref[...] = jnp.full(m_ref.shape, NEG, F32)
    l_ref[...] = jnp.zeros_like(l_ref)
    acc_ref[...] = jnp.zeros_like(acc_ref)


def _fold_rows(x, op):
    r, t = x.shape
    x = x.reshape(r // SUBLANES, SUBLANES, t)
    n = r // SUBLANES
    while n > 1:
        assert n % 2 == 0
        n //= 2
        x = op(x[:n], x[n:])
    return x[0]


def _reduce_rows(x, op):
    x = _fold_rows(x, op)
    n = SUBLANES
    while n > 1:
        n //= 2
        x = op(x[:n], x[n:])
    return x


def _softmax_update(st, hd, m_ref, l_ref):
    m_prev = m_ref[hd]
    m_next = jnp.maximum(m_prev, _reduce_rows(st, jnp.maximum))
    alpha = jnp.exp2(m_prev - m_next)
    p = jnp.exp2(st - m_next)
    l_ref[hd] = alpha * l_ref[hd] + _reduce_rows(p, jnp.add)
    m_ref[hd] = m_next
    return alpha, p.astype(BF16)


def _attend_heads(n_heads, logits_fn, value_fn, m_ref, l_ref, acc_ref):
    st = logits_fn(0)
    for hd in range(n_heads):
        st_next = logits_fn(hd + 1) if hd + 1 < n_heads else None
        alpha, p = _softmax_update(st, hd, m_ref, l_ref)
        acc_ref[hd] = alpha * acc_ref[hd] + _dot(value_fn(hd), p)
        st = st_next


def _causal_pairs(nt):
    it = [i for i in range(nt) for _ in range(i + 1)]
    jt = [j for i in range(nt) for j in range(i + 1)]
    return jnp.asarray(it, I32), jnp.asarray(jt, I32)


_TOK_Q = lambda bi, p, it, jt: (bi, it[p], 0)
_FM_Q = lambda bi, p, it, jt: (bi, 0, it[p])
_TOK_K = lambda bi, p, it, jt: (bi, jt[p], 0)
_FM_K = lambda bi, p, it, jt: (bi, 0, jt[p])


def _causal_tile(tk, tq, k0, q0):
    kpos = k0 + lax.broadcasted_iota(I32, (tk, tq), 0)
    qpos = q0 + lax.broadcasted_iota(I32, (tk, tq), 1)
    return kpos <= qpos


def _diff_attn_kernel(lambda_init, it_ref, jt_ref, qt_ref, k_ref, vt_ref, lq1_ref, lk1_ref,
                      lq2_ref, lk2_ref, subg_ref, o_ref, qz_ref, m_ref, l_ref, acc_ref):
    i, j = it_ref[pl.program_id(1)], jt_ref[pl.program_id(1)]
    tq, tk = qt_ref.shape[2], k_ref.shape[1]

    @pl.when(j == 0)
    def _():
        _fill_masked_queries(qt_ref, qz_ref, 2 * DIFF_HEADS)
        _init_state(m_ref, l_ref, acc_ref)

    def step(diagonal):
        if diagonal:
            causal = _causal_tile(tk, tq, 0, 0)

        def logits(mp):
            st = _dot(k_ref[0, :, (mp // 2) * LANES:(mp // 2 + 1) * LANES], qz_ref[mp])
            return jnp.where(causal, st, NEG) if diagonal else st

        def values(mp):
            return vt_ref[0, (mp // 2) * LANES:(mp // 2 + 1) * LANES, :]

        _attend_heads(2 * DIFF_HEADS, logits, values, m_ref, l_ref, acc_ref)

    @pl.when(j < i)
    def _():
        step(False)

    @pl.when(j == i)
    def _():
        step(True)
        lam = (jnp.exp(jnp.sum(lq1_ref[...] * lk1_ref[...], axis=1, keepdims=True))
               - jnp.exp(jnp.sum(lq2_ref[...] * lk2_ref[...], axis=1, keepdims=True))
               + lambda_init)
        for hd in range(DIFF_HEADS):
            o = acc_ref[2 * hd] / l_ref[2 * hd] - lam * (acc_ref[2 * hd + 1] / l_ref[2 * hd + 1])
            o = o * lax.rsqrt(jnp.mean(o * o, axis=0, keepdims=True) + EPS)
            o = o.T * (subg_ref[...] * (1.0 - lambda_init))
            o_ref[0, :, hd * LANES:(hd + 1) * LANES] = o.astype(BF16)


def _diff_attn(aqt, ak, avt, lq1, lk1, lq2, lk2, sub_g, lambda_init, b, s):
    t = min(L0_TILE, s)
    it, jt = _causal_pairs(s // t)
    const = lambda bi, p, it, jt: (0, 0)
    vec = lambda a: a.astype(F32).reshape(1, -1)
    n_maps = 2 * DIFF_HEADS
    grid_spec = pltpu.PrefetchScalarGridSpec(
        num_scalar_prefetch=2,
        grid=(b, it.shape[0]),
        in_specs=[pl.BlockSpec((1, 512, t), _FM_Q), pl.BlockSpec((1, t, 512), _TOK_K),
                  pl.BlockSpec((1, 512, t), _FM_K),
                  pl.BlockSpec((1, HEAD_DIM), const), pl.BlockSpec((1, HEAD_DIM), const),
                  pl.BlockSpec((1, HEAD_DIM), const), pl.BlockSpec((1, HEAD_DIM), const),
                  pl.BlockSpec((1, LANES), const)],
        out_specs=pl.BlockSpec((1, t, 512), _TOK_Q),
        scratch_shapes=[pltpu.VMEM((n_maps, LANES, t), BF16),
                        pltpu.VMEM((n_maps, 1, t), F32),
                        pltpu.VMEM((n_maps, 1, t), F32),
                        pltpu.VMEM((n_maps, LANES, t), F32)])
    return pl.pallas_call(
        functools.partial(_diff_attn_kernel, lambda_init),
        grid_spec=grid_spec,
        out_shape=jax.ShapeDtypeStruct((b, s, 512), BF16),
        compiler_params=_params(("arbitrary", "arbitrary")),
        name="diff_attn",
    )(it, jt, aqt, ak.reshape(b, s, 512), avt, vec(lq1), vec(lk1), vec(lq2), vec(lk2),
      vec(sub_g))


def _fox_attn_kernel(it_ref, jt_ref, qt_ref, k_ref, vt_ref, ck_ref, cq_ref, o_ref, qz_ref,
                     m_ref, l_ref, acc_ref):
    i, j = it_ref[pl.program_id(1)], jt_ref[pl.program_id(1)]
    tq, tk = qt_ref.shape[2], k_ref.shape[1]

    @pl.when(j == 0)
    def _():
        _fill_masked_queries(qt_ref, qz_ref, FOX_HEADS)
        _init_state(m_ref, l_ref, acc_ref)

    def step(diagonal):
        if diagonal:
            causal = _causal_tile(tk, tq, 0, 0)
        decay = (cq_ref[0, 0:1, :] - ck_ref[0]) * LOG2E

        def logits(hd):
            st = _dot(k_ref[0, :, (hd // 2) * LANES:(hd // 2 + 1) * LANES], qz_ref[hd])
            st = st + decay[:, hd:hd + 1]
            return jnp.where(causal, st, NEG) if diagonal else st

        def values(hd):
            return vt_ref[0, hd * HEAD_DIM:(hd + 1) * HEAD_DIM, :]

        _attend_heads(FOX_HEADS, logits, values, m_ref, l_ref, acc_ref)

    @pl.when(j < i)
    def _():
        step(False)

    @pl.when(j == i)
    def _():
        step(True)
        for grp in range(FOX_HEADS // 2):
            o = jnp.concatenate([acc_ref[2 * grp] / l_ref[2 * grp],
                                 acc_ref[2 * grp + 1] / l_ref[2 * grp + 1]], axis=0)
            o_ref[0, :, grp * LANES:(grp + 1) * LANES] = o.T.astype(BF16)


def _fox_attn(fqt, fk, fvt, c, b, s):
    t = min(L0_TILE, s)
    it, jt = _causal_pairs(s // t)
    first_rows = lambda bi, p, it, jt: (bi, it[p] * (t // SUBLANES), 0)
    grid_spec = pltpu.PrefetchScalarGridSpec(
        num_scalar_prefetch=2,
        grid=(b, it.shape[0]),
        in_specs=[pl.BlockSpec((1, 512, t), _FM_Q), pl.BlockSpec((1, t, 512), _TOK_K),
                  pl.BlockSpec((1, 512, t), _FM_K), pl.BlockSpec((1, t, LANES), _TOK_K),
                  pl.BlockSpec((1, SUBLANES, LANES), first_rows)],
        out_specs=pl.BlockSpec((1, t, 512), _TOK_Q),
        scratch_shapes=[pltpu.VMEM((FOX_HEADS, LANES, t), BF16),
                        pltpu.VMEM((FOX_HEADS, 1, t), F32),
                        pltpu.VMEM((FOX_HEADS, 1, t), F32),
                        pltpu.VMEM((FOX_HEADS, HEAD_DIM, t), F32)])
    return pl.pallas_call(
        _fox_attn_kernel,
        grid_spec=grid_spec,
        out_shape=jax.ShapeDtypeStruct((b, s, 512), BF16),
        compiler_params=_params(("arbitrary", "arbitrary")),
        name="fox_attn",
    )(it, jt, fqt, fk.reshape(b, s, 512), fvt, c, c)


def _mlp_kernel(n_mix, final, *refs):
    h_ref = refs[0]
    mix_refs = refs[1:1 + n_mix]
    wo_refs = refs[1 + n_mix:1 + 2 * n_mix]
    g_ref, w1_ref, w2_ref, gf_ref, out_ref, h1_ref, hn_ref, acc_ref = refs[1 + 2 * n_mix:]
    f = pl.program_id(1)

    @pl.when(f == 0)
    def _():
        h1 = h_ref[...]
        for mix_ref, wo_ref in zip(mix_refs, wo_refs):
            h1 = h1 + _dot(mix_ref[...], wo_ref[...])
        h1_ref[...] = h1
        hn_ref[...] = _rms(h1, g_ref[...]).astype(BF16)
        acc_ref[...] = jnp.zeros_like(acc_ref)

    u = _dot(hn_ref[...], w1_ref[...])
    a = jnp.square(jnp.maximum(u, 0.0)).astype(BF16)
    acc_ref[...] += _dot(a, w2_ref[...])

    @pl.when(f == pl.num_programs(1) - 1)
    def _():
        y = h1_ref[...] + acc_ref[...]
        if final:
            y = _rms(y, gf_ref[...])
        out_ref[...] = y


def _outproj_mlp(h2d, mixes, w_outs, g, w1, w2, g_final, final, s):
    n, d = h2d.shape
    dff = w1.shape[1]
    tm = min(ROW_TILE, s)
    tf = min(FF_TILE, dff)
    row = lambda i, f: (i, 0)
    const = lambda i, f: (0, 0)
    in_specs = [pl.BlockSpec((tm, d), row)]
    in_specs += [pl.BlockSpec((tm, m.shape[1]), row) for m in mixes]
    in_specs += [pl.BlockSpec(w.shape, const) for w in w_outs]
    in_specs += [pl.BlockSpec((1, d), const),
                 pl.BlockSpec((d, tf), lambda i, f: (0, f)),
                 pl.BlockSpec((tf, d), lambda i, f: (f, 0)),
                 pl.BlockSpec((1, d), const)]
    return pl.pallas_call(
        functools.partial(_mlp_kernel, len(mixes), final),
        grid=(n // tm, dff // tf),
        in_specs=in_specs,
        out_specs=pl.BlockSpec((tm, d), row),
        out_shape=jax.ShapeDtypeStruct((n, d), F32),
        scratch_shapes=[pltpu.VMEM((tm, d), F32), pltpu.VMEM((tm, d), BF16),
                        pltpu.VMEM((tm, d), F32)],
        compiler_params=_params(("arbitrary", "arbitrary")),
        name="outproj_mlp",
    )(h2d, *mixes, *[w.astype(BF16) for w in w_outs], g.reshape(1, d), w1.astype(BF16),
      w2.astype(BF16), g_final.reshape(1, d))


def _odd_proj_kernel(x_ref, g_ref, w_ref, wt_ref, lng_ref, lnb_ref, c_ref, s1_ref, s2_ref,
                     cos_ref, sin_ref, k_ref, ik_ref, qt_ref, vt_ref, iqt_ref, iwt_ref):
    hn = _rms(x_ref[...], g_ref[...]).astype(BF16)
    c, s1, s2 = c_ref[...], s1_ref[...], s2_ref[...]
    for grp in range(DSA_HEADS // 2):
        lo = grp * LANES
        k = _dot(hn, w_ref[:, lo:lo + LANES])
        k_ref[:, lo:lo + LANES] = _rope_lanes(k, c, s1, s2).astype(BF16)
    ik = _dot(hn, w_ref[:, 1024:1024 + LANES])
    mu = jnp.mean(ik, axis=1, keepdims=True)
    var = jnp.mean(jnp.square(ik - mu), axis=1, keepdims=True)
    ik = (ik - mu) * lax.rsqrt(var + EPS) * lng_ref[...] + lnb_ref[...]
    ik_ref[...] = _rope_lanes(ik, c, s1, s2).astype(BF16)
    cos_t, sin_t = cos_ref[...], sin_ref[...]
    for hd in range(DSA_HEADS):
        lo = hd * HEAD_DIM
        qt = _dot_nt(wt_ref[lo:lo + HEAD_DIM, :], hn)
        qt_ref[0, lo:lo + HEAD_DIM, :] = (_rope_rows(qt, cos_t, sin_t) * Q_SCALE).astype(BF16)
    vt_ref[0] = _dot_nt(wt_ref[1024:2048, :], hn).astype(BF16)
    for hd in range(IDX_HEADS):
        lo = hd * HEAD_DIM
        iqt = _dot_nt(wt_ref[2048 + lo:2048 + lo + HEAD_DIM, :], hn)
        iqt = _rope_rows(iqt, cos_t, sin_t) * HEAD_DIM ** -0.5
        iqt_ref[0, lo:lo + HEAD_DIM, :] = iqt.astype(BF16)
    iwt = _dot_nt(wt_ref[2560:2560 + BF16_ROWS, :], hn)
    iwt_ref[0] = iwt[0:IDX_HEADS] * IDX_HEADS ** -0.5


def _odd_proj(h2d, g, w_in, ln_g, ln_b, tabs, b, s):
    n, d = h2d.shape
    tm = min(ROW_TILE, s)
    nb = s // tm
    wq, wk, wv, wiq, wik, wiw = (
        w_in[:, 0:1024], w_in[:, 1024:2048], w_in[:, 2048:3072], w_in[:, 3072:3584],
        w_in[:, 3584:3648], w_in[:, 3648:3656])
    wiw = jnp.pad(wiw, ((0, 0), (0, BF16_ROWS - IDX_HEADS)))
    w = jnp.concatenate([wk, wik, wik], axis=1).astype(BF16)
    wt = jnp.concatenate([wq, wv, wiq, wiw], axis=1).T.astype(BF16)
    two = lambda a: jnp.concatenate([a, a]).astype(F32).reshape(1, LANES)
    c, s1, s2, cos_t, sin_t = tabs
    row = lambda i: (i, 0)
    pos = lambda i: (i % nb, 0)
    post = lambda i: (0, i % nb)
    fmaj = lambda i: (i // nb, 0, i % nb)
    const = lambda i: (0, 0)
    return pl.pallas_call(
        _odd_proj_kernel,
        grid=(n // tm,),
        in_specs=[
            pl.BlockSpec((tm, d), row), pl.BlockSpec((1, d), const),
            pl.BlockSpec(w.shape, const), pl.BlockSpec(wt.shape, const),
            pl.BlockSpec((1, LANES), const), pl.BlockSpec((1, LANES), const),
            pl.BlockSpec((tm, LANES), pos), pl.BlockSpec((tm, LANES), pos),
            pl.BlockSpec((tm, LANES), pos),
            pl.BlockSpec((ROT_HALF, tm), post), pl.BlockSpec((ROT_HALF, tm), post)],
        out_specs=(
            pl.BlockSpec((tm, 1024), row), pl.BlockSpec((tm, LANES), row),
            pl.BlockSpec((1, 1024, tm), fmaj), pl.BlockSpec((1, 1024, tm), fmaj),
            pl.BlockSpec((1, 512, tm), fmaj), pl.BlockSpec((1, IDX_HEADS, tm), fmaj)),
        out_shape=(
            jax.ShapeDtypeStruct((n, 1024), BF16), jax.ShapeDtypeStruct((n, LANES), BF16),
            jax.ShapeDtypeStruct((b, 1024, s), BF16), jax.ShapeDtypeStruct((b, 1024, s), BF16),
            jax.ShapeDtypeStruct((b, 512, s), BF16), jax.ShapeDtypeStruct((b, IDX_HEADS, s), F32)),
        compiler_params=_params(("arbitrary",)),
        name="odd_proj",
    )(h2d, g.reshape(1, d), w, wt, two(ln_g), two(ln_b), c, s1, s2, cos_t, sin_t)


def _dsa_attn_kernel(k_sel, it_ref, jt_ref, qt_ref, k_ref, vt_ref, iqt_ref, ik_ref, iwt_ref,
                     o_ref, keys_ref, thr_ref, need_ref, tie_ref, flag_ref, bias_ref,
                     qz_ref, iqz_ref, m_ref, l_ref, acc_ref):
    i, j = it_ref[pl.program_id(1)], jt_ref[pl.program_id(1)]
    tq, tk = qt_ref.shape[2], k_ref.shape[1]
    j_last = i

    def count_ge(cand):
        def body(c, part):
            return part + _fold_rows(jnp.where(keys_ref[c] >= cand, 1, 0), jnp.add)
        part = lax.fori_loop(0, j_last + 1, body, jnp.zeros((SUBLANES, tq), I32))
        return jnp.sum(part, axis=0, keepdims=True)

    @pl.when(j == 0)
    def _():
        _fill_masked_queries(qt_ref, qz_ref, DSA_HEADS)
        _fill_masked_queries(iqt_ref, iqz_ref, IDX_HEADS)
        _init_state(m_ref, l_ref, acc_ref)
        tie_ref[...] = jnp.zeros_like(tie_ref)
        iw = iwt_ref[0]

        def score_tile(c, carry):
            ik = ik_ref[0, pl.ds(pl.multiple_of(c * tk, tk), tk), :]
            score = jnp.zeros((tk, tq), F32)
            for hd in range(IDX_HEADS):
                score = score + jnp.maximum(_dot(ik, iqz_ref[hd]), 0.0) * iw[hd:hd + 1, :]
            score = jnp.where(score == 0.0, 0.0, score)
            bits = lax.bitcast_convert_type(score, I32)
            key = bits ^ ((bits >> 31) & 0x7FFFFFFF)
            keys_ref[c] = jnp.where(_causal_tile(tk, tq, c * tk, i * tq), key, INT_MIN)
            return carry
        lax.fori_loop(0, j_last + 1, score_tile, 0)

        def bit_step(step, thr):
            cand = thr + lax.shift_left(jnp.int32(1), 31 - step)
            return jnp.where(count_ge(cand) >= k_sel, cand, thr)
        thr = lax.fori_loop(0, 32, bit_step, jnp.full((1, tq), INT_MIN, I32))
        n_ge = count_ge(thr)
        n_gt = count_ge(thr + 1)
        real = thr > INT_MIN
        thr_ref[...] = thr
        need_ref[...] = jnp.where(real, k_sel - n_gt, 0).astype(F32)
        flag_ref[0] = jnp.max(jnp.where(real & (n_ge > k_sel), 1, 0))

    thr = thr_ref[...]
    kc = keys_ref[j]

    @pl.when(flag_ref[0] == 0)
    def _():
        bias_ref[...] = jnp.where(kc >= jnp.maximum(thr, INT_MIN + 1), 0.0, NEG)

    @pl.when(flag_ref[0] != 0)
    def _():
        gt = kc > thr
        eq = (kc == thr) & (thr > INT_MIN)
        earlier = (lax.broadcasted_iota(I32, (tk, tk), 1) < lax.broadcasted_iota(I32, (tk, tk), 0))
        eq_f = jnp.where(eq, 1.0, 0.0)
        rank = _dot(jnp.where(earlier, 1.0, 0.0).astype(BF16), eq_f.astype(BF16)) + tie_ref[...]
        sel = gt | (eq & (rank < need_ref[...]))
        bias_ref[...] = jnp.where(sel, 0.0, NEG)
        tie_ref[...] = tie_ref[...] + jnp.sum(eq_f, axis=0, keepdims=True)

    def logits(hd):
        k_grp = k_ref[0, :, (hd // 2) * LANES:(hd // 2 + 1) * LANES]
        return _dot(k_grp, qz_ref[hd]) + bias_ref[...]

    def values(hd):
        return vt_ref[0, hd * HEAD_DIM:(hd + 1) * HEAD_DIM, :]

    _attend_heads(DSA_HEADS, logits, values, m_ref, l_ref, acc_ref)

    @pl.when(j == j_last)
    def _():
        for grp in range(DSA_HEADS // 2):
            o = jnp.concatenate([acc_ref[2 * grp] / l_ref[2 * grp],
                                 acc_ref[2 * grp + 1] / l_ref[2 * grp + 1]], axis=0)
            o_ref[0, :, grp * LANES:(grp + 1) * LANES] = o.T.astype(BF16)


def _dsa_attn(qt, k, vt, iqt, ik, iwt, b, s):
    t = min(DSA_TILE, s)
    nt = s // t
    it, jt = _causal_pairs(nt)
    k_sel = min(TOPK_MAX, s // 4)
    assert t >= k_sel
    tq = tk = t
    grid_spec = pltpu.PrefetchScalarGridSpec(
        num_scalar_prefetch=2,
        grid=(b, it.shape[0]),
        in_specs=[pl.BlockSpec((1, 1024, t), _FM_Q), pl.BlockSpec((1, t, 1024), _TOK_K),
                  pl.BlockSpec((1, 1024, t), _FM_K), pl.BlockSpec((1, 512, t), _FM_Q),
                  pl.BlockSpec((1, s, LANES), lambda bi, p, it, jt: (bi, 0, 0)),
                  pl.BlockSpec((1, IDX_HEADS, t), _FM_Q)],
        out_specs=pl.BlockSpec((1, t, 1024), _TOK_Q),
        scratch_shapes=[pltpu.VMEM((nt, tk, tq), I32),
                        pltpu.VMEM((1, tq), I32),
                        pltpu.VMEM((1, tq), F32),
                        pltpu.VMEM((1, tq), F32),
                        pltpu.SMEM((1,), I32),
                        pltpu.VMEM((tk, tq), F32),
                        pltpu.VMEM((DSA_HEADS, LANES, tq), BF16),
                        pltpu.VMEM((IDX_HEADS, LANES, tq), BF16),
                        pltpu.VMEM((DSA_HEADS, 1, tq), F32),
                        pltpu.VMEM((DSA_HEADS, 1, tq), F32),
                        pltpu.VMEM((DSA_HEADS, HEAD_DIM, tq), F32)])
    return pl.pallas_call(
        functools.partial(_dsa_attn_kernel, k_sel),
        grid_spec=grid_spec,
        out_shape=jax.ShapeDtypeStruct((b, s, 1024), BF16),
        compiler_params=_params(("arbitrary", "arbitrary")),
        name="dsa_attn",
    )(it, jt, qt, k.reshape(b, s, 1024), vt, iqt, ik.reshape(b, s, LANES), iwt)


def _rope_tables(s):
    pos = jnp.arange(s, dtype=F32)
    inv_freq = ROPE_THETA ** (-jnp.arange(0, ROT_DIM, 2, dtype=F32) / ROT_DIM)
    ang = pos[:, None] * inv_freq[None, :]
    cos, sin = jnp.cos(ang), jnp.sin(ang)
    zeros = jnp.zeros((s, HEAD_DIM - ROT_DIM), F32)
    zero8 = jnp.zeros((s, ROT_HALF), F32)
    c = jnp.concatenate([cos, cos, zeros + 1.0], axis=1)
    s1 = jnp.concatenate([zero8, sin, zeros], axis=1)
    s2 = jnp.concatenate([-sin, zero8, zeros], axis=1)
    two = lambda a: jnp.concatenate([a, a], axis=1)
    return two(c), two(s1), two(s2), cos.T, sin.T


def kernel(x, norm_mix, w_in_even, b_forget, lambda_q1, lambda_k1, lambda_q2, lambda_k2,
           diff_subln_g, w_out_even, w_in_odd, idx_ln_g, idx_ln_b, w_out_odd, norm_mlp,
           w_mlp_in, w_mlp_out, norm_final):
    b, s, d = x.shape
    tabs = _rope_tables(s)
    h = x.reshape(b * s, d)

    lambda_init = 0.8 - 0.6 * math.exp(-0.3 * 0)
    ak, fk, fl, aqt, avt, fqt, fvt = _even_proj(h, norm_mix[0], w_in_even[0], tabs, b, s)
    c = _fox_cumsum(fl, b_forget[0], b, s)
    oa = _diff_attn(aqt, ak, avt, lambda_q1[0], lambda_k1[0], lambda_q2[0], lambda_k2[0],
                    diff_subln_g[0], lambda_init, b, s)
    ob = _fox_attn(fqt, fk, fvt, c, b, s)
    h = _outproj_mlp(h, [oa.reshape(b * s, 512), ob.reshape(b * s, 512)],
                     [w_out_even[0][0:512], w_out_even[0][512:1024]], norm_mlp[0],
                     w_mlp_in[0], w_mlp_out[0], norm_final, False, s)

    k, ik, qt, vt, iqt, iwt = _odd_proj(h, norm_mix[1], w_in_odd[0], idx_ln_g[0], idx_ln_b[0],
                                        tabs, b, s)
    o = _dsa_attn(qt, k, vt, iqt, ik, iwt, b, s)
    h = _outproj_mlp(h, [o.reshape(b * s, 1024)], [w_out_odd[0]], norm_mlp[1],
                     w_mlp_in[1], w_mlp_out[1], norm_final, True, s)
    return h.reshape(b, s, d)
```

```python
import functools
import math

import jax
import jax.numpy as jnp
from jax import lax
from jax.experimental import pallas as pl
from jax.experimental.pallas import tpu as pltpu

F32 = jnp.float32
BF16 = jnp.bfloat16
I32 = jnp.int32

HEAD_DIM = 64
ROT_DIM = HEAD_DIM // 4
ROT_HALF = ROT_DIM // 2
ROPE_THETA = 500000.0
EPS = 1e-6
DIFF_HEADS = 4
FOX_HEADS = 8
DSA_HEADS = 16
IDX_HEADS = 8
TOPK_MAX = 256

LANES = 128
SUBLANES = 8
BF16_ROWS = 16
NEG = -1e30
MAX_SLACK = 16.0
INT_MIN = -(2 ** 31)
LOG2E = math.log2(math.e)
Q_SCALE = HEAD_DIM ** -0.5 * LOG2E
VMEM_LIMIT = 56 * 1024 * 1024

ROW_TILE = 512
FF_TILE = 1024
CUMSUM_TILE = 256
L0_TILE = 512
DSA_TILE = 512


def _params(sem):
    return pltpu.CompilerParams(dimension_semantics=sem, vmem_limit_bytes=VMEM_LIMIT)


def _rms(x, g):
    return x * lax.rsqrt(jnp.mean(x * x, axis=1, keepdims=True) + EPS) * g


def _dot(a, b):
    return jnp.dot(a, b, preferred_element_type=F32)


def _dot_nt(a, b):
    return lax.dot_general(a, b, (((1,), (1,)), ((), ())), preferred_element_type=F32)


def _rope_lanes(x, c, s1, s2):
    return x * c + pltpu.roll(x, ROT_HALF, 1) * s1 + pltpu.roll(x, LANES - ROT_HALF, 1) * s2


def _rope_rows(xh, cos_t, sin_t):
    x1 = xh[0:ROT_HALF]
    x2 = xh[ROT_HALF:ROT_DIM]
    return jnp.concatenate(
        [x1 * cos_t - x2 * sin_t, x2 * cos_t + x1 * sin_t, xh[ROT_DIM:]], axis=0)


def _even_proj_kernel(x_ref, g_ref, w_ref, wt_ref, c_ref, s1_ref, s2_ref, cos_ref, sin_ref,
                      ak_ref, fk_ref, fl_ref, aqt_ref, avt_ref, fqt_ref, fvt_ref):
    hn = _rms(x_ref[...], g_ref[...]).astype(BF16)
    c, s1, s2 = c_ref[...], s1_ref[...], s2_ref[...]
    ak = _dot(hn, w_ref[:, 0:512])
    for grp in range(DIFF_HEADS):
        lo = grp * LANES
        ak_ref[:, lo:lo + LANES] = _rope_lanes(ak[:, lo:lo + LANES], c, s1, s2).astype(BF16)
    fk_ref[...] = _dot(hn, w_ref[:, 512:1024]).astype(BF16)
    fl_ref[...] = _dot(hn, w_ref[:, 1024:1024 + LANES])
    cos_t, sin_t = cos_ref[...], sin_ref[...]
    aqt = _dot_nt(wt_ref[0:512, :], hn)
    for hd in range(2 * DIFF_HEADS):
        lo = hd * HEAD_DIM
        qt = _rope_rows(aqt[lo:lo + HEAD_DIM], cos_t, sin_t) * Q_SCALE
        aqt_ref[0, lo:lo + HEAD_DIM, :] = qt.astype(BF16)
    avt_ref[0] = _dot_nt(wt_ref[512:1024, :], hn).astype(BF16)
    fqt_ref[0] = (_dot_nt(wt_ref[1024:1536, :], hn) * Q_SCALE).astype(BF16)
    fvt_ref[0] = _dot_nt(wt_ref[1536:2048, :], hn).astype(BF16)


def _even_proj(h2d, g, w_in, tabs, b, s):
    n, d = h2d.shape
    tm = min(ROW_TILE, s)
    nb = s // tm
    a_q, a_k, a_v, f_q, f_k, f_v, f_l = (
        w_in[:, 0:512], w_in[:, 512:1024], w_in[:, 1024:1536], w_in[:, 1536:2048],
        w_in[:, 2048:2560], w_in[:, 2560:3072], w_in[:, 3072:3080])
    f_l = jnp.pad(f_l, ((0, 0), (0, LANES - FOX_HEADS)))
    w = jnp.concatenate([a_k, f_k, f_l], axis=1).astype(BF16)
    wt = jnp.concatenate([a_q, a_v, f_q, f_v], axis=1).T.astype(BF16)
    c, s1, s2, cos_t, sin_t = tabs
    row = lambda i: (i, 0)
    pos = lambda i: (i % nb, 0)
    post = lambda i: (0, i % nb)
    fmaj = lambda i: (i // nb, 0, i % nb)
    const = lambda i: (0, 0)
    fm_shape = jax.ShapeDtypeStruct((b, 512, s), BF16)
    fm_spec = pl.BlockSpec((1, 512, tm), fmaj)
    return pl.pallas_call(
        _even_proj_kernel,
        grid=(n // tm,),
        in_specs=[
            pl.BlockSpec((tm, d), row), pl.BlockSpec((1, d), const),
            pl.BlockSpec(w.shape, const), pl.BlockSpec(wt.shape, const),
            pl.BlockSpec((tm, LANES), pos), pl.BlockSpec((tm, LANES), pos),
            pl.BlockSpec((tm, LANES), pos),
            pl.BlockSpec((ROT_HALF, tm), post), pl.BlockSpec((ROT_HALF, tm), post)],
        out_specs=(
            pl.BlockSpec((tm, 512), row), pl.BlockSpec((tm, 512), row),
            pl.BlockSpec((tm, LANES), row), fm_spec, fm_spec, fm_spec, fm_spec),
        out_shape=(
            jax.ShapeDtypeStruct((n, 512), BF16), jax.ShapeDtypeStruct((n, 512), BF16),
            jax.ShapeDtypeStruct((n, LANES), F32), fm_shape, fm_shape, fm_shape, fm_shape),
        compiler_params=_params(("arbitrary",)),
        name="even_proj",
    )(h2d, g.reshape(1, d), w, wt, c, s1, s2, cos_t, sin_t)


def _fox_cumsum_kernel(fl_ref, bf_ref, c_ref, ct_ref, carry_ref):
    @pl.when(pl.program_id(1) == 0)
    def _():
        carry_ref[...] = jnp.zeros_like(carry_ref)

    z = fl_ref[0] + bf_ref[...]
    logf = jnp.minimum(z, 0.0) - jnp.log1p(jnp.exp(-jnp.abs(z)))
    t = z.shape[0]
    tri = (lax.broadcasted_iota(I32, (t, t), 0) >= lax.broadcasted_iota(I32, (t, t), 1)).astype(F32)
    cs = jnp.dot(tri, logf, precision=lax.Precision.HIGHEST,
                 preferred_element_type=F32) + carry_ref[...]
    c_ref[0] = cs
    ct_ref[0] = cs.T[0:FOX_HEADS, :]
    carry_ref[...] = cs[t - 1:t, :]


def _fox_cumsum(fl, b_f, b, s):
    tc = min(CUMSUM_TILE, s)
    bf = jnp.pad(b_f.astype(F32), (0, LANES - FOX_HEADS)).reshape(1, LANES)
    return pl.pallas_call(
        _fox_cumsum_kernel,
        grid=(b, s // tc),
        in_specs=[pl.BlockSpec((1, tc, LANES), lambda bi, i: (bi, i, 0)),
                  pl.BlockSpec((1, LANES), lambda bi, i: (0, 0))],
        out_specs=(pl.BlockSpec((1, tc, LANES), lambda bi, i: (bi, i, 0)),
                   pl.BlockSpec((1, FOX_HEADS, tc), lambda bi, i: (bi, 0, i))),
        out_shape=(jax.ShapeDtypeStruct((b, s, LANES), F32),
                   jax.ShapeDtypeStruct((b, FOX_HEADS, s), F32)),
        scratch_shapes=[pltpu.VMEM((1, LANES), F32)],
        compiler_params=_params(("arbitrary", "arbitrary")),
        name="fox_cumsum",
    )(fl.reshape(b, s, LANES), bf)


def _fill_masked_queries(qt_ref, qz_ref, n_heads):
    row = lax.broadcasted_iota(I32, (LANES, qt_ref.shape[2]), 0)
    for hd in range(n_heads):
        grp = hd // 2
        qg = qt_ref[0, grp * LANES:(grp + 1) * LANES, :]
        keep = (row < HEAD_DIM) if hd % 2 == 0 else (row >= HEAD_DIM)
        qz_ref[hd] = jnp.where(keep, qg, jnp.zeros_like(qg))


def _init_state(m_ref, l_ref, acc_ref):
    m_ref[...] = jnp.full(m_ref.shape, NEG, F32)
    l_ref[...] = jnp.zeros_like(l_ref)
    acc_ref[...] = jnp.zeros_like(acc_ref)


def _fold_rows(x, op):
    r, t = x.shape
    x = x.reshape(r // SUBLANES, SUBLANES, t)
    n = r // SUBLANES
    while n > 1:
        assert n % 2 == 0
        n //= 2
        x = op(x[:n], x[n:])
    return x[0]


def _reduce_rows(x, op):
    x = _fold_rows(x, op)
    n = SUBLANES
    while n > 1:
        n //= 2
        x = op(x[:n], x[n:])
    return x


def _softmax_update(st, hd, m_ref, l_ref):
    m_prev = m_ref[hd]
    m_next = jnp.maximum(m_prev, _reduce_rows(st, jnp.maximum))
    alpha = jnp.exp2(m_prev - m_next)
    p = jnp.exp2(st - m_next)
    l_ref[hd] = alpha * l_ref[hd] + _reduce_rows(p, jnp.add)
    m_ref[hd] = m_next
    return alpha, p.astype(BF16)


def _attend_heads(n_heads, logits_fn, value_fn, m_ref, l_ref, acc_ref):
    st = logits_fn(0)
    for hd in range(n_heads):
        st_next = logits_fn(hd + 1) if hd + 1 < n_heads else None
        alpha, p = _softmax_update(st, hd, m_ref, l_ref)
        acc_ref[hd] = alpha * acc_ref[hd] + _dot(value_fn(hd), p)
        st = st_next


def _weight_rows(w_row):
    rows = lax.broadcasted_iota(I32, (BF16_ROWS, w_row.shape[1]), 0)
    return jnp.where(rows == 0, w_row, 0.0).astype(BF16)


def _attend_heads_lazy(n_heads, raw_logits_fn, weighted_value_fn, keep, logits_fn, value_fn,
                       m_ref, l_ref, acc_ref, pv_ref):
    d = acc_ref.shape[1]
    excess = jnp.full((1, m_ref.shape[2]), NEG, F32)
    st = raw_logits_fn(0)
    for hd in range(n_heads):
        st_next = raw_logits_fn(hd + 1) if hd + 1 < n_heads else None
        m_prev = m_ref[hd]
        excess = jnp.maximum(excess, _reduce_rows(st, jnp.maximum) - m_prev)
        p = jnp.exp2(st - m_prev).astype(BF16)
        pv_ref[hd] = _dot(weighted_value_fn(hd), p if keep is None else p * keep)
        st = st_next
    within = jnp.max(excess) <= MAX_SLACK

    @pl.when(within)
    def _():
        for hd in range(n_heads):
            acc_ref[hd] = acc_ref[hd] + pv_ref[hd, 0:d]
            l_ref[hd] = l_ref[hd] + pv_ref[hd, d:d + 1]

    @pl.when(jnp.logical_not(within))
    def _():
        _attend_heads(n_heads, logits_fn, value_fn, m_ref, l_ref, acc_ref)


def _causal_pairs(nt):
    it = [i for i in range(nt) for _ in range(i + 1)]
    jt = [j for i in range(nt) for j in range(i, -1, -1)]
    return jnp.asarray(it, I32), jnp.asarray(jt, I32)


_TOK_Q = lambda bi, p, it, jt: (bi, it[p], 0)
_FM_Q = lambda bi, p, it, jt: (bi, 0, it[p])
_TOK_K = lambda bi, p, it, jt: (bi, jt[p], 0)
_FM_K = lambda bi, p, it, jt: (bi, 0, jt[p])


def _causal_tile(tk, tq, k0, q0):
    kpos = k0 + lax.broadcasted_iota(I32, (tk, tq), 0)
    qpos = q0 + lax.broadcasted_iota(I32, (tk, tq), 1)
    return kpos <= qpos


def _diff_attn_kernel(lambda_init, it_ref, jt_ref, qt_ref, k_ref, vt_ref, lq1_ref, lk1_ref,
                      lq2_ref, lk2_ref, subg_ref, o_ref, qz_ref, m_ref, l_ref, acc_ref, pv_ref):
    i, j = it_ref[pl.program_id(1)], jt_ref[pl.program_id(1)]
    tq, tk = qt_ref.shape[2], k_ref.shape[1]
    n_maps = 2 * DIFF_HEADS

    def values(mp):
        return vt_ref[0, (mp // 2) * LANES:(mp // 2 + 1) * LANES, :]

    @pl.when(j == i)
    def _():
        _fill_masked_queries(qt_ref, qz_ref, n_maps)
        _init_state(m_ref, l_ref, acc_ref)
        causal = _causal_tile(tk, tq, 0, 0)

        def logits(mp):
            st = _dot(k_ref[0, :, (mp // 2) * LANES:(mp // 2 + 1) * LANES], qz_ref[mp])
            return jnp.where(causal, st, NEG)

        _attend_heads(n_maps, logits, values, m_ref, l_ref, acc_ref)

    @pl.when(j < i)
    def _():
        ones = _weight_rows(jnp.ones((1, tk), F32))

        def logits(mp):
            return _dot(k_ref[0, :, (mp // 2) * LANES:(mp // 2 + 1) * LANES], qz_ref[mp])

        def weighted_values(mp):
            return jnp.concatenate([values(mp), ones], axis=0)

        _attend_heads_lazy(n_maps, logits, weighted_values, None, logits, values,
                           m_ref, l_ref, acc_ref, pv_ref)

    @pl.when(j == 0)
    def _():
        lam = (jnp.exp(jnp.sum(lq1_ref[...] * lk1_ref[...], axis=1, keepdims=True))
               - jnp.exp(jnp.sum(lq2_ref[...] * lk2_ref[...], axis=1, keepdims=True))
               + lambda_init)
        for hd in range(DIFF_HEADS):
            o = acc_ref[2 * hd] / l_ref[2 * hd] - lam * (acc_ref[2 * hd + 1] / l_ref[2 * hd + 1])
            o = o * lax.rsqrt(jnp.mean(o * o, axis=0, keepdims=True) + EPS)
            o = o.T * (subg_ref[...] * (1.0 - lambda_init))
            o_ref[0, :, hd * LANES:(hd + 1) * LANES] = o.astype(BF16)


def _diff_attn(aqt, ak, avt, lq1, lk1, lq2, lk2, sub_g, lambda_init, b, s):
    t = min(L0_TILE, s)
    it, jt = _causal_pairs(s // t)
    const = lambda bi, p, it, jt: (0, 0)
    vec = lambda a: a.astype(F32).reshape(1, -1)
    n_maps = 2 * DIFF_HEADS
    grid_spec = pltpu.PrefetchScalarGridSpec(
        num_scalar_prefetch=2,
        grid=(b, it.shape[0]),
        in_specs=[pl.BlockSpec((1, 512, t), _FM_Q), pl.BlockSpec((1, t, 512), _TOK_K),
                  pl.BlockSpec((1, 512, t), _FM_K),
                  pl.BlockSpec((1, HEAD_DIM), const), pl.BlockSpec((1, HEAD_DIM), const),
                  pl.BlockSpec((1, HEAD_DIM), const), pl.BlockSpec((1, HEAD_DIM), const),
                  pl.BlockSpec((1, LANES), const)],
        out_specs=pl.BlockSpec((1, t, 512), _TOK_Q),
        scratch_shapes=[pltpu.VMEM((n_maps, LANES, t), BF16),
                        pltpu.VMEM((n_maps, 1, t), F32),
                        pltpu.VMEM((n_maps, 1, t), F32),
                        pltpu.VMEM((n_maps, LANES, t), F32),
                        pltpu.VMEM((n_maps, LANES + BF16_ROWS, t), F32)])
    return pl.pallas_call(
        functools.partial(_diff_attn_kernel, lambda_init),
        grid_spec=grid_spec,
        out_shape=jax.ShapeDtypeStruct((b, s, 512), BF16),
        compiler_params=_params(("arbitrary", "arbitrary")),
        name="diff_attn",
    )(it, jt, aqt, ak.reshape(b, s, 512), avt, vec(lq1), vec(lk1), vec(lq2), vec(lk2),
      vec(sub_g))


def _fox_attn_kernel(it_ref, jt_ref, qt_ref, k_ref, vt_ref, ck_ref, cq_ref, ctk_ref, ctq_ref,
                     o_ref, qz_ref, m_ref, l_ref, acc_ref, pv_ref):
    i, j = it_ref[pl.program_id(1)], jt_ref[pl.program_id(1)]
    tq, tk = qt_ref.shape[2], k_ref.shape[1]

    def values(hd):
        return vt_ref[0, hd * HEAD_DIM:(hd + 1) * HEAD_DIM, :]

    def make_logits(causal):
        decay = (cq_ref[0, 0:1, :] - ck_ref[0]) * LOG2E

        def logits(hd):
            st = _dot(k_ref[0, :, (hd // 2) * LANES:(hd // 2 + 1) * LANES], qz_ref[hd])
            st = st + decay[:, hd:hd + 1]
            return st if causal is None else jnp.where(causal, st, NEG)
        return logits

    @pl.when(j == i)
    def _():
        _fill_masked_queries(qt_ref, qz_ref, FOX_HEADS)
        _init_state(m_ref, l_ref, acc_ref)
        logits = make_logits(_causal_tile(tk, tq, 0, 0))
        _attend_heads(FOX_HEADS, logits, values, m_ref, l_ref, acc_ref)

    @pl.when(j < i)
    def _():
        w = jnp.exp2((ctq_ref[0, :, 0:1] - ctk_ref[0]) * LOG2E)

        def raw_logits(hd):
            return _dot(k_ref[0, :, (hd // 2) * LANES:(hd // 2 + 1) * LANES], qz_ref[hd])

        def weighted_values(hd):
            w_row = w[hd:hd + 1]
            return jnp.concatenate([values(hd) * w_row.astype(BF16), _weight_rows(w_row)], axis=0)

        _attend_heads_lazy(FOX_HEADS, raw_logits, weighted_values, None, make_logits(None),
                           values, m_ref, l_ref, acc_ref, pv_ref)

    @pl.when(j == 0)
    def _():
        for grp in range(FOX_HEADS // 2):
            o = jnp.concatenate([acc_ref[2 * grp] / l_ref[2 * grp],
                                 acc_ref[2 * grp + 1] / l_ref[2 * grp + 1]], axis=0)
            o_ref[0, :, grp * LANES:(grp + 1) * LANES] = o.T.astype(BF16)


def _fox_attn(fqt, fk, fvt, c, ct, b, s):
    t = min(L0_TILE, s)
    it, jt = _causal_pairs(s // t)
    first_rows = lambda bi, p, it, jt: (bi, it[p] * (t // SUBLANES), 0)
    grid_spec = pltpu.PrefetchScalarGridSpec(
        num_scalar_prefetch=2,
        grid=(b, it.shape[0]),
        in_specs=[pl.BlockSpec((1, 512, t), _FM_Q), pl.BlockSpec((1, t, 512), _TOK_K),
                  pl.BlockSpec((1, 512, t), _FM_K), pl.BlockSpec((1, t, LANES), _TOK_K),
                  pl.BlockSpec((1, SUBLANES, LANES), first_rows),
                  pl.BlockSpec((1, FOX_HEADS, t), _FM_K), pl.BlockSpec((1, FOX_HEADS, t), _FM_Q)],
        out_specs=pl.BlockSpec((1, t, 512), _TOK_Q),
        scratch_shapes=[pltpu.VMEM((FOX_HEADS, LANES, t), BF16),
                        pltpu.VMEM((FOX_HEADS, 1, t), F32),
                        pltpu.VMEM((FOX_HEADS, 1, t), F32),
                        pltpu.VMEM((FOX_HEADS, HEAD_DIM, t), F32),
                        pltpu.VMEM((FOX_HEADS, HEAD_DIM + BF16_ROWS, t), F32)])
    return pl.pallas_call(
        _fox_attn_kernel,
        grid_spec=grid_spec,
        out_shape=jax.ShapeDtypeStruct((b, s, 512), BF16),
        compiler_params=_params(("arbitrary", "arbitrary")),
        name="fox_attn",
    )(it, jt, fqt, fk.reshape(b, s, 512), fvt, c, c, ct, ct)


def _mlp_kernel(n_mix, final, *refs):
    h_ref = refs[0]
    mix_refs = refs[1:1 + n_mix]
    wo_refs = refs[1 + n_mix:1 + 2 * n_mix]
    g_ref, w1_ref, w2_ref, gf_ref, out_ref, h1_ref, hn_ref, acc_ref = refs[1 + 2 * n_mix:]
    f = pl.program_id(1)

    @pl.when(f == 0)
    def _():
        h1 = h_ref[...]
        for mix_ref, wo_ref in zip(mix_refs, wo_refs):
            h1 = h1 + _dot(mix_ref[...], wo_ref[...])
        h1_ref[...] = h1
        hn_ref[...] = _rms(h1, g_ref[...]).astype(BF16)
        acc_ref[...] = jnp.zeros_like(acc_ref)

    u = _dot(hn_ref[...], w1_ref[...])
    a = jnp.square(jnp.maximum(u, 0.0)).astype(BF16)
    acc_ref[...] += _dot(a, w2_ref[...])

    @pl.when(f == pl.num_programs(1) - 1)
    def _():
        y = h1_ref[...] + acc_ref[...]
        if final:
            y = _rms(y, gf_ref[...])
        out_ref[...] = y


def _outproj_mlp(h2d, mixes, w_outs, g, w1, w2, g_final, final, s):
    n, d = h2d.shape
    dff = w1.shape[1]
    tm = min(ROW_TILE, s)
    tf = min(FF_TILE, dff)
    row = lambda i, f: (i, 0)
    const = lambda i, f: (0, 0)
    in_specs = [pl.BlockSpec((tm, d), row)]
    in_specs += [pl.BlockSpec((tm, m.shape[1]), row) for m in mixes]
    in_specs += [pl.BlockSpec(w.shape, const) for w in w_outs]
    in_specs += [pl.BlockSpec((1, d), const),
                 pl.BlockSpec((d, tf), lambda i, f: (0, f)),
                 pl.BlockSpec((tf, d), lambda i, f: (f, 0)),
                 pl.BlockSpec((1, d), const)]
    return pl.pallas_call(
        functools.partial(_mlp_kernel, len(mixes), final),
        grid=(n // tm, dff // tf),
        in_specs=in_specs,
        out_specs=pl.BlockSpec((tm, d), row),
        out_shape=jax.ShapeDtypeStruct((n, d), F32),
        scratch_shapes=[pltpu.VMEM((tm, d), F32), pltpu.VMEM((tm, d), BF16),
                        pltpu.VMEM((tm, d), F32)],
        compiler_params=_params(("arbitrary", "arbitrary")),
        name="outproj_mlp",
    )(h2d, *mixes, *[w.astype(BF16) for w in w_outs], g.reshape(1, d), w1.astype(BF16),
      w2.astype(BF16), g_final.reshape(1, d))


def _odd_proj_kernel(x_ref, g_ref, w_ref, wt_ref, lng_ref, lnb_ref, c_ref, s1_ref, s2_ref,
                     cos_ref, sin_ref, k_ref, ik_ref, qt_ref, vt_ref, iqt_ref, iwt_ref):
    hn = _rms(x_ref[...], g_ref[...]).astype(BF16)
    c, s1, s2 = c_ref[...], s1_ref[...], s2_ref[...]
    k = _dot(hn, w_ref[:, 0:1024])
    for grp in range(DSA_HEADS // 2):
        lo = grp * LANES
        k_ref[:, lo:lo + LANES] = _rope_lanes(k[:, lo:lo + LANES], c, s1, s2).astype(BF16)
    ik = _dot(hn, w_ref[:, 1024:1024 + LANES])
    mu = jnp.mean(ik, axis=1, keepdims=True)
    var = jnp.mean(jnp.square(ik - mu), axis=1, keepdims=True)
    ik = (ik - mu) * lax.rsqrt(var + EPS) * lng_ref[...] + lnb_ref[...]
    ik_ref[...] = _rope_lanes(ik, c, s1, s2).astype(BF16)
    cos_t, sin_t = cos_ref[...], sin_ref[...]
    qt_all = _dot_nt(wt_ref[0:1024, :], hn)
    for hd in range(DSA_HEADS):
        lo = hd * HEAD_DIM
        qt = _rope_rows(qt_all[lo:lo + HEAD_DIM], cos_t, sin_t) * Q_SCALE
        qt_ref[0, lo:lo + HEAD_DIM, :] = qt.astype(BF16)
    vt_ref[0] = _dot_nt(wt_ref[1024:2048, :], hn).astype(BF16)
    iqt_all = _dot_nt(wt_ref[2048:2560 + BF16_ROWS, :], hn)
    for hd in range(IDX_HEADS):
        lo = hd * HEAD_DIM
        iqt = _rope_rows(iqt_all[lo:lo + HEAD_DIM], cos_t, sin_t) * HEAD_DIM ** -0.5
        iqt_ref[0, lo:lo + HEAD_DIM, :] = iqt.astype(BF16)
    iwt_ref[0] = iqt_all[512:512 + IDX_HEADS] * IDX_HEADS ** -0.5


def _odd_proj(h2d, g, w_in, ln_g, ln_b, tabs, b, s):
    n, d = h2d.shape
    tm = min(ROW_TILE, s)
    nb = s // tm
    wq, wk, wv, wiq, wik, wiw = (
        w_in[:, 0:1024], w_in[:, 1024:2048], w_in[:, 2048:3072], w_in[:, 3072:3584],
        w_in[:, 3584:3648], w_in[:, 3648:3656])
    wiw = jnp.pad(wiw, ((0, 0), (0, BF16_ROWS - IDX_HEADS)))
    w = jnp.concatenate([wk, wik, wik], axis=1).astype(BF16)
    wt = jnp.concatenate([wq, wv, wiq, wiw], axis=1).T.astype(BF16)
    two = lambda a: jnp.concatenate([a, a]).astype(F32).reshape(1, LANES)
    c, s1, s2, cos_t, sin_t = tabs
    row = lambda i: (i, 0)
    pos = lambda i: (i % nb, 0)
    post = lambda i: (0, i % nb)
    fmaj = lambda i: (i // nb, 0, i % nb)
    const = lambda i: (0, 0)
    return pl.pallas_call(
        _odd_proj_kernel,
        grid=(n // tm,),
        in_specs=[
            pl.BlockSpec((tm, d), row), pl.BlockSpec((1, d), const),
            pl.BlockSpec(w.shape, const), pl.BlockSpec(wt.shape, const),
            pl.BlockSpec((1, LANES), const), pl.BlockSpec((1, LANES), const),
            pl.BlockSpec((tm, LANES), pos), pl.BlockSpec((tm, LANES), pos),
            pl.BlockSpec((tm, LANES), pos),
            pl.BlockSpec((ROT_HALF, tm), post), pl.BlockSpec((ROT_HALF, tm), post)],
        out_specs=(
            pl.BlockSpec((tm, 1024), row), pl.BlockSpec((tm, LANES), row),
            pl.BlockSpec((1, 1024, tm), fmaj), pl.BlockSpec((1, 1024, tm), fmaj),
            pl.BlockSpec((1, 512, tm), fmaj), pl.BlockSpec((1, IDX_HEADS, tm), fmaj)),
        out_shape=(
            jax.ShapeDtypeStruct((n, 1024), BF16), jax.ShapeDtypeStruct((n, LANES), BF16),
            jax.ShapeDtypeStruct((b, 1024, s), BF16), jax.ShapeDtypeStruct((b, 1024, s), BF16),
            jax.ShapeDtypeStruct((b, 512, s), BF16), jax.ShapeDtypeStruct((b, IDX_HEADS, s), F32)),
        compiler_params=_params(("arbitrary",)),
        name="odd_proj",
    )(h2d, g.reshape(1, d), w, wt, two(ln_g), two(ln_b), c, s1, s2, cos_t, sin_t)


def _dsa_attn_kernel(k_sel, it_ref, jt_ref, qt_ref, k_ref, vt_ref, iqt_ref, ik_ref, iwt_ref,
                     o_ref, keys_ref, thr_ref, need_ref, tie_ref, flag_ref, bias_ref,
                     keep_ref, qz_ref, iqz_ref, m_ref, l_ref, acc_ref, pv_ref):
    i, j = it_ref[pl.program_id(1)], jt_ref[pl.program_id(1)]
    tq, tk = qt_ref.shape[2], k_ref.shape[1]

    def count_ge(cand):
        def body(c, part):
            return part + _fold_rows(jnp.where(keys_ref[c] >= cand, 1, 0), jnp.add)
        part = lax.fori_loop(0, i + 1, body, jnp.zeros((SUBLANES, tq), I32))
        return jnp.sum(part, axis=0, keepdims=True)

    @pl.when(j == i)
    def _():
        _fill_masked_queries(qt_ref, qz_ref, DSA_HEADS)
        _fill_masked_queries(iqt_ref, iqz_ref, IDX_HEADS)
        _init_state(m_ref, l_ref, acc_ref)
        iw = iwt_ref[0]

        def score_tile(c, carry):
            ik = ik_ref[0, pl.ds(pl.multiple_of(c * tk, tk), tk), :]
            score = jnp.zeros((tk, tq), F32)
            for hd in range(IDX_HEADS):
                score = score + jnp.maximum(_dot(ik, iqz_ref[hd]), 0.0) * iw[hd:hd + 1, :]
            score = jnp.where(score == 0.0, 0.0, score)
            bits = lax.bitcast_convert_type(score, I32)
            key = bits ^ ((bits >> 31) & 0x7FFFFFFF)
            keys_ref[c] = jnp.where(_causal_tile(tk, tq, c * tk, i * tq), key, INT_MIN)
            return carry
        lax.fori_loop(0, i + 1, score_tile, 0)

        def bit_step(step, thr):
            cand = thr + lax.shift_left(jnp.int32(1), 31 - step)
            return jnp.where(count_ge(cand) >= k_sel, cand, thr)
        thr = lax.fori_loop(0, 32, bit_step, jnp.full((1, tq), INT_MIN, I32))
        n_ge = count_ge(thr)
        n_gt = count_ge(thr + 1)
        real = thr > INT_MIN
        thr_ref[...] = thr
        need_ref[...] = jnp.where(real, k_sel - n_gt, 0).astype(F32)
        flag_ref[0] = jnp.max(jnp.where(real & (n_ge > k_sel), 1, 0))

        @pl.when(flag_ref[0] != 0)
        def _():
            def tie_tile(c, seen):
                tie_ref[c] = seen
                hit = (keys_ref[c] == thr) & real
                return seen + jnp.sum(jnp.where(hit, 1.0, 0.0), axis=0, keepdims=True)
            lax.fori_loop(0, i + 1, tie_tile, jnp.zeros((1, tq), F32))

    thr = thr_ref[...]
    kc = keys_ref[j]

    def store_selection(sel):
        bias_ref[...] = jnp.where(sel, 0.0, NEG)
        keep_ref[...] = jnp.where(sel, 1.0, 0.0).astype(BF16)

    @pl.when(flag_ref[0] == 0)
    def _():
        store_selection(kc >= jnp.maximum(thr, INT_MIN + 1))

    @pl.when(flag_ref[0] != 0)
    def _():
        gt = kc > thr
        eq = (kc == thr) & (thr > INT_MIN)
        earlier = (lax.broadcasted_iota(I32, (tk, tk), 1) < lax.broadcasted_iota(I32, (tk, tk), 0))
        eq_f = jnp.where(eq, 1.0, 0.0).astype(BF16)
        rank = _dot(jnp.where(earlier, 1.0, 0.0).astype(BF16), eq_f) + tie_ref[j]
        store_selection(gt | (eq & (rank < need_ref[...])))

    def raw_logits(hd):
        return _dot(k_ref[0, :, (hd // 2) * LANES:(hd // 2 + 1) * LANES], qz_ref[hd])

    def logits(hd):
        return raw_logits(hd) + bias_ref[...]

    def values(hd):
        return vt_ref[0, hd * HEAD_DIM:(hd + 1) * HEAD_DIM, :]

    @pl.when(j == i)
    def _():
        _attend_heads(DSA_HEADS, logits, values, m_ref, l_ref, acc_ref)

    @pl.when(j < i)
    def _():
        ones = _weight_rows(jnp.ones((1, tk), F32))

        def weighted_values(hd):
            return jnp.concatenate([values(hd), ones], axis=0)

        _attend_heads_lazy(DSA_HEADS, raw_logits, weighted_values, keep_ref[...], logits, values,
                           m_ref, l_ref, acc_ref, pv_ref)

    @pl.when(j == 0)
    def _():
        for grp in range(DSA_HEADS // 2):
            o = jnp.concatenate([acc_ref[2 * grp] / l_ref[2 * grp],
                                 acc_ref[2 * grp + 1] / l_ref[2 * grp + 1]], axis=0)
            o_ref[0, :, grp * LANES:(grp + 1) * LANES] = o.T.astype(BF16)


def _dsa_attn(qt, k, vt, iqt, ik, iwt, b, s):
    t = min(DSA_TILE, s)
    nt = s // t
    it, jt = _causal_pairs(nt)
    k_sel = min(TOPK_MAX, s // 4)
    assert t >= k_sel
    tq = tk = t
    grid_spec = pltpu.PrefetchScalarGridSpec(
        num_scalar_prefetch=2,
        grid=(b, it.shape[0]),
        in_specs=[pl.BlockSpec((1, 1024, t), _FM_Q), pl.BlockSpec((1, t, 1024), _TOK_K),
                  pl.BlockSpec((1, 1024, t), _FM_K), pl.BlockSpec((1, 512, t), _FM_Q),
                  pl.BlockSpec((1, s, LANES), lambda bi, p, it, jt: (bi, 0, 0)),
                  pl.BlockSpec((1, IDX_HEADS, t), _FM_Q)],
        out_specs=pl.BlockSpec((1, t, 1024), _TOK_Q),
        scratch_shapes=[pltpu.VMEM((nt, tk, tq), I32),
                        pltpu.VMEM((1, tq), I32),
                        pltpu.VMEM((1, tq), F32),
                        pltpu.VMEM((nt, 1, tq), F32),
                        pltpu.SMEM((1,), I32),
                        pltpu.VMEM((tk, tq), F32),
                        pltpu.VMEM((tk, tq), BF16),
                        pltpu.VMEM((DSA_HEADS, LANES, tq), BF16),
                        pltpu.VMEM((IDX_HEADS, LANES, tq), BF16),
                        pltpu.VMEM((DSA_HEADS, 1, tq), F32),
                        pltpu.VMEM((DSA_HEADS, 1, tq), F32),
                        pltpu.VMEM((DSA_HEADS, HEAD_DIM, tq), F32),
                        pltpu.VMEM((DSA_HEADS, HEAD_DIM + BF16_ROWS, tq), F32)])
    return pl.pallas_call(
        functools.partial(_dsa_attn_kernel, k_sel),
        grid_spec=grid_spec,
        out_shape=jax.ShapeDtypeStruct((b, s, 1024), BF16),
        compiler_params=_params(("arbitrary", "arbitrary")),
        name="dsa_attn",
    )(it, jt, qt, k.reshape(b, s, 1024), vt, iqt, ik.reshape(b, s, LANES), iwt)


def _rope_tables(s):
    pos = jnp.arange(s, dtype=F32)
    inv_freq = ROPE_THETA ** (-jnp.arange(0, ROT_DIM, 2, dtype=F32) / ROT_DIM)
    ang = pos[:, None] * inv_freq[None, :]
    cos, sin = jnp.cos(ang), jnp.sin(ang)
    zeros = jnp.zeros((s, HEAD_DIM - ROT_DIM), F32)
    zero8 = jnp.zeros((s, ROT_HALF), F32)
    c = jnp.concatenate([cos, cos, zeros + 1.0], axis=1)
    s1 = jnp.concatenate([zero8, sin, zeros], axis=1)
    s2 = jnp.concatenate([-sin, zero8, zeros], axis=1)
    two = lambda a: jnp.concatenate([a, a], axis=1)
    return two(c), two(s1), two(s2), cos.T, sin.T


def kernel(x, norm_mix, w_in_even, b_forget, lambda_q1, lambda_k1, lambda_q2, lambda_k2,
           diff_subln_g, w_out_even, w_in_odd, idx_ln_g, idx_ln_b, w_out_odd, norm_mlp,
           w_mlp_in, w_mlp_out, norm_final):
    b, s, d = x.shape
    tabs = _rope_tables(s)
    h = x.reshape(b * s, d)

    lambda_init = 0.8 - 0.6 * math.exp(-0.3 * 0)
    ak, fk, fl, aqt, avt, fqt, fvt = _even_proj(h, norm_mix[0], w_in_even[0], tabs, b, s)
    c, ct = _fox_cumsum(fl, b_forget[0], b, s)
    oa = _diff_attn(aqt, ak, avt, lambda_q1[0], lambda_k1[0], lambda_q2[0], lambda_k2[0],
                    diff_subln_g[0], lambda_init, b, s)
    ob = _fox_attn(fqt, fk, fvt, c, ct, b, s)
    h = _outproj_mlp(h, [oa.reshape(b * s, 512), ob.reshape(b * s, 512)],
                     [w_out_even[0][0:512], w_out_even[0][512:1024]], norm_mlp[0],
                     w_mlp_in[0], w_mlp_out[0], norm_final, False, s)

    k, ik, qt, vt, iqt, iwt = _odd_proj(h, norm_mix[1], w_in_odd[0], idx_ln_g[0], idx_ln_b[0],
                                        tabs, b, s)
    o = _dsa_attn(qt, k, vt, iqt, ik, iwt, b, s)
    h = _outproj_mlp(h, [o.reshape(b * s, 1024)], [w_out_odd[0]], norm_mlp[1],
                     w_mlp_in[1], w_mlp_out[1], norm_final, True, s)
    return h.reshape(b, s, d)
```

```python
import functools
import math

import jax
import jax.numpy as jnp
from jax import lax
from jax.experimental import pallas as pl
from jax.experimental.pallas import tpu as pltpu

F32 = jnp.float32
BF16 = jnp.bfloat16
I32 = jnp.int32

HEAD_DIM = 64
ROT_DIM = HEAD_DIM // 4
ROT_HALF = ROT_DIM // 2
ROPE_THETA = 500000.0
EPS = 1e-6
DIFF_HEADS = 4
FOX_HEADS = 8
DSA_HEADS = 16
IDX_HEADS = 8
TOPK_MAX = 256

LANES = 128
SUBLANES = 8
BF16_ROWS = 16
NEG = -1e30
MAX_SLACK = 16.0
INT_MIN = -(2 ** 31)
NEG_INF_ORDINAL = INT_MIN + 0x7FFFFF
LOG2E = math.log2(math.e)
Q_SCALE = HEAD_DIM ** -0.5 * LOG2E
VMEM_LIMIT = 56 * 1024 * 1024

ROW_TILE = 512
FF_TILE = 1024
CUMSUM_TILE = 256
L0_TILE = 512
DSA_TILE = 512


def _params(sem):
    return pltpu.CompilerParams(dimension_semantics=sem, vmem_limit_bytes=VMEM_LIMIT)


def _rms(x, g):
    return x * lax.rsqrt(jnp.mean(x * x, axis=1, keepdims=True) + EPS) * g


def _dot(a, b):
    return jnp.dot(a, b, preferred_element_type=F32)


def _dot_nt(a, b):
    return lax.dot_general(a, b, (((1,), (1,)), ((), ())), preferred_element_type=F32)


def _rope_lanes(x, c, s1, s2):
    return x * c + pltpu.roll(x, ROT_HALF, 1) * s1 + pltpu.roll(x, LANES - ROT_HALF, 1) * s2


def _rope_rows(xh, cos_t, sin_t):
    x1 = xh[0:ROT_HALF]
    x2 = xh[ROT_HALF:ROT_DIM]
    return jnp.concatenate(
        [x1 * cos_t - x2 * sin_t, x2 * cos_t + x1 * sin_t, xh[ROT_DIM:]], axis=0)


def _even_proj_kernel(x_ref, g_ref, w_ref, wt_ref, c_ref, s1_ref, s2_ref, cos_ref, sin_ref,
                      ak_ref, fk_ref, fl_ref, aqt_ref, avt_ref, fqt_ref, fvt_ref):
    hn = _rms(x_ref[...], g_ref[...]).astype(BF16)
    c, s1, s2 = c_ref[...], s1_ref[...], s2_ref[...]
    ak = _dot(hn, w_ref[:, 0:512])
    for grp in range(DIFF_HEADS):
        lo = grp * LANES
        ak_ref[:, lo:lo + LANES] = _rope_lanes(ak[:, lo:lo + LANES], c, s1, s2).astype(BF16)
    fk_ref[...] = _dot(hn, w_ref[:, 512:1024]).astype(BF16)
    fl_ref[...] = _dot(hn, w_ref[:, 1024:1024 + LANES])
    cos_t, sin_t = cos_ref[...], sin_ref[...]
    aqt = _dot_nt(wt_ref[0:512, :], hn)
    for hd in range(2 * DIFF_HEADS):
        lo = hd * HEAD_DIM
        qt = _rope_rows(aqt[lo:lo + HEAD_DIM], cos_t, sin_t) * Q_SCALE
        aqt_ref[0, lo:lo + HEAD_DIM, :] = qt.astype(BF16)
    avt_ref[0] = _dot_nt(wt_ref[512:1024, :], hn).astype(BF16)
    fqt_ref[0] = (_dot_nt(wt_ref[1024:1536, :], hn) * Q_SCALE).astype(BF16)
    fvt_ref[0] = _dot_nt(wt_ref[1536:2048, :], hn).astype(BF16)


def _even_proj(h2d, g, w_in, tabs, b, s):
    n, d = h2d.shape
    tm = min(ROW_TILE, s)
    nb = s // tm
    a_q, a_k, a_v, f_q, f_k, f_v, f_l = (
        w_in[:, 0:512], w_in[:, 512:1024], w_in[:, 1024:1536], w_in[:, 1536:2048],
        w_in[:, 2048:2560], w_in[:, 2560:3072], w_in[:, 3072:3080])
    f_l = jnp.pad(f_l, ((0, 0), (0, LANES - FOX_HEADS)))
    w = jnp.concatenate([a_k, f_k, f_l], axis=1).astype(BF16)
    wt = jnp.concatenate([a_q, a_v, f_q, f_v], axis=1).T.astype(BF16)
    c, s1, s2, cos_t, sin_t = tabs
    row = lambda i: (i, 0)
    pos = lambda i: (i % nb, 0)
    post = lambda i: (0, i % nb)
    fmaj = lambda i: (i // nb, 0, i % nb)
    const = lambda i: (0, 0)
    fm_shape = jax.ShapeDtypeStruct((b, 512, s), BF16)
    fm_spec = pl.BlockSpec((1, 512, tm), fmaj)
    return pl.pallas_call(
        _even_proj_kernel,
        grid=(n // tm,),
        in_specs=[
            pl.BlockSpec((tm, d), row), pl.BlockSpec((1, d), const),
            pl.BlockSpec(w.shape, const), pl.BlockSpec(wt.shape, const),
            pl.BlockSpec((tm, LANES), pos), pl.BlockSpec((tm, LANES), pos),
            pl.BlockSpec((tm, LANES), pos),
            pl.BlockSpec((ROT_HALF, tm), post), pl.BlockSpec((ROT_HALF, tm), post)],
        out_specs=(
            pl.BlockSpec((tm, 512), row), pl.BlockSpec((tm, 512), row),
            pl.BlockSpec((tm, LANES), row), fm_spec, fm_spec, fm_spec, fm_spec),
        out_shape=(
            jax.ShapeDtypeStruct((n, 512), BF16), jax.ShapeDtypeStruct((n, 512), BF16),
            jax.ShapeDtypeStruct((n, LANES), F32), fm_shape, fm_shape, fm_shape, fm_shape),
        compiler_params=_params(("arbitrary",)),
        name="even_proj",
    )(h2d, g.reshape(1, d), w, wt, c, s1, s2, cos_t, sin_t)


def _fox_cumsum_kernel(fl_ref, bf_ref, c_ref, ct_ref, carry_ref):
    @pl.when(pl.program_id(1) == 0)
    def _():
        carry_ref[...] = jnp.zeros_like(carry_ref)

    z = fl_ref[0] + bf_ref[...]
    logf = jnp.minimum(z, 0.0) - jnp.log1p(jnp.exp(-jnp.abs(z)))
    t = z.shape[0]
    tri = (lax.broadcasted_iota(I32, (t, t), 0) >= lax.broadcasted_iota(I32, (t, t), 1)).astype(F32)
    cs = jnp.dot(tri, logf, precision=lax.Precision.HIGHEST,
                 preferred_element_type=F32) + carry_ref[...]
    c_ref[0] = cs
    ct_ref[0] = cs.T[0:FOX_HEADS, :]
    carry_ref[...] = cs[t - 1:t, :]


def _fox_cumsum(fl, b_f, b, s):
    tc = min(CUMSUM_TILE, s)
    bf = jnp.pad(b_f.astype(F32), (0, LANES - FOX_HEADS)).reshape(1, LANES)
    return pl.pallas_call(
        _fox_cumsum_kernel,
        grid=(b, s // tc),
        in_specs=[pl.BlockSpec((1, tc, LANES), lambda bi, i: (bi, i, 0)),
                  pl.BlockSpec((1, LANES), lambda bi, i: (0, 0))],
        out_specs=(pl.BlockSpec((1, tc, LANES), lambda bi, i: (bi, i, 0)),
                   pl.BlockSpec((1, FOX_HEADS, tc), lambda bi, i: (bi, 0, i))),
        out_shape=(jax.ShapeDtypeStruct((b, s, LANES), F32),
                   jax.ShapeDtypeStruct((b, FOX_HEADS, s), F32)),
        scratch_shapes=[pltpu.VMEM((1, LANES), F32)],
        compiler_params=_params(("arbitrary", "arbitrary")),
        name="fox_cumsum",
    )(fl.reshape(b, s, LANES), bf)


def _fill_masked_queries(qt_ref, qz_ref, n_heads):
    row = lax.broadcasted_iota(I32, (LANES, qt_ref.shape[2]), 0)
    for hd in range(n_heads):
        grp = hd // 2
        qg = qt_ref[0, grp * LANES:(grp + 1) * LANES, :]
        keep = (row < HEAD_DIM) if hd % 2 == 0 else (row >= HEAD_DIM)
        qz_ref[hd] = jnp.where(keep, qg, jnp.zeros_like(qg))


def _init_state(m_ref, l_ref, acc_ref):
    m_ref[...] = jnp.full(m_ref.shape, NEG, F32)
    l_ref[...] = jnp.zeros_like(l_ref)
    acc_ref[...] = jnp.zeros_like(acc_ref)


def _fold_rows(x, op, group=SUBLANES):
    r, t = x.shape
    x = x.reshape(r // group, group, t)
    n = r // group
    while n > 1:
        assert n % 2 == 0
        n //= 2
        x = op(x[:n], x[n:])
    return x[0]


def _reduce_rows(x, op):
    x = _fold_rows(x, op)
    n = SUBLANES
    while n > 1:
        n //= 2
        x = op(x[:n], x[n:])
    return x


def _softmax_update(st, hd, m_ref, l_ref):
    m_prev = m_ref[hd]
    m_next = jnp.maximum(m_prev, _reduce_rows(st, jnp.maximum))
    alpha = jnp.exp2(m_prev - m_next)
    p = jnp.exp2(st - m_next)
    l_ref[hd] = alpha * l_ref[hd] + _reduce_rows(p, jnp.add)
    m_ref[hd] = m_next
    return alpha, p.astype(BF16)


def _attend_heads(n_heads, logits_fn, value_fn, m_ref, l_ref, acc_ref):
    st = logits_fn(0)
    for hd in range(n_heads):
        st_next = logits_fn(hd + 1) if hd + 1 < n_heads else None
        alpha, p = _softmax_update(st, hd, m_ref, l_ref)
        acc_ref[hd] = alpha * acc_ref[hd] + _dot(value_fn(hd), p)
        st = st_next


def _weight_rows(w_row):
    rows = lax.broadcasted_iota(I32, (BF16_ROWS, w_row.shape[1]), 0)
    return jnp.where(rows == 0, w_row, 0.0).astype(BF16)


def _attend_heads_lazy(n_heads, raw_logits_fn, weighted_value_fn, keep, logits_fn, value_fn,
                       m_ref, l_ref, acc_ref, pv_ref):
    d = acc_ref.shape[1]
    excess = jnp.full((1, m_ref.shape[2]), NEG, F32)
    st = raw_logits_fn(0)
    for hd in range(n_heads):
        st_next = raw_logits_fn(hd + 1) if hd + 1 < n_heads else None
        m_prev = m_ref[hd]
        excess = jnp.maximum(excess, _reduce_rows(st, jnp.maximum) - m_prev)
        p = jnp.exp2(st - m_prev).astype(BF16)
        pv_ref[hd] = _dot(weighted_value_fn(hd), p if keep is None else p * keep)
        st = st_next
    within = jnp.max(excess) <= MAX_SLACK

    @pl.when(within)
    def _():
        for hd in range(n_heads):
            acc_ref[hd] = acc_ref[hd] + pv_ref[hd, 0:d]
            l_ref[hd] = l_ref[hd] + pv_ref[hd, d:d + 1]

    @pl.when(jnp.logical_not(within))
    def _():
        _attend_heads(n_heads, logits_fn, value_fn, m_ref, l_ref, acc_ref)


def _causal_pairs(nt):
    it = [i for i in range(nt) for _ in range(i + 1)]
    jt = [j for i in range(nt) for j in range(i, -1, -1)]
    return jnp.asarray(it, I32), jnp.asarray(jt, I32)


_TOK_Q = lambda bi, p, it, jt: (bi, it[p], 0)
_FM_Q = lambda bi, p, it, jt: (bi, 0, it[p])
_TOK_K = lambda bi, p, it, jt: (bi, jt[p], 0)
_FM_K = lambda bi, p, it, jt: (bi, 0, jt[p])


def _causal_tile(tk, tq, k0, q0):
    kpos = k0 + lax.broadcasted_iota(I32, (tk, tq), 0)
    qpos = q0 + lax.broadcasted_iota(I32, (tk, tq), 1)
    return kpos <= qpos


def _diff_attn_kernel(lambda_init, it_ref, jt_ref, qt_ref, k_ref, vt_ref, lq1_ref, lk1_ref,
                      lq2_ref, lk2_ref, subg_ref, o_ref, qz_ref, m_ref, l_ref, acc_ref, pv_ref):
    i, j = it_ref[pl.program_id(1)], jt_ref[pl.program_id(1)]
    tq, tk = qt_ref.shape[2], k_ref.shape[1]
    n_maps = 2 * DIFF_HEADS

    def values(mp):
        return vt_ref[0, (mp // 2) * LANES:(mp // 2 + 1) * LANES, :]

    @pl.when(j == i)
    def _():
        _fill_masked_queries(qt_ref, qz_ref, n_maps)
        _init_state(m_ref, l_ref, acc_ref)
        causal = _causal_tile(tk, tq, 0, 0)

        def logits(mp):
            st = _dot(k_ref[0, :, (mp // 2) * LANES:(mp // 2 + 1) * LANES], qz_ref[mp])
            return jnp.where(causal, st, NEG)

        _attend_heads(n_maps, logits, values, m_ref, l_ref, acc_ref)

    @pl.when(j < i)
    def _():
        ones = _weight_rows(jnp.ones((1, tk), F32))

        def logits(mp):
            return _dot(k_ref[0, :, (mp // 2) * LANES:(mp // 2 + 1) * LANES], qz_ref[mp])

        def weighted_values(mp):
            return jnp.concatenate([values(mp), ones], axis=0)

        _attend_heads_lazy(n_maps, logits, weighted_values, None, logits, values,
                           m_ref, l_ref, acc_ref, pv_ref)

    @pl.when(j == 0)
    def _():
        lam = (jnp.exp(jnp.sum(lq1_ref[...] * lk1_ref[...], axis=1, keepdims=True))
               - jnp.exp(jnp.sum(lq2_ref[...] * lk2_ref[...], axis=1, keepdims=True))
               + lambda_init)
        for hd in range(DIFF_HEADS):
            o = acc_ref[2 * hd] / l_ref[2 * hd] - lam * (acc_ref[2 * hd + 1] / l_ref[2 * hd + 1])
            o = o * lax.rsqrt(jnp.mean(o * o, axis=0, keepdims=True) + EPS)
            o = o.T * (subg_ref[...] * (1.0 - lambda_init))
            o_ref[0, :, hd * LANES:(hd + 1) * LANES] = o.astype(BF16)


def _diff_attn(aqt, ak, avt, lq1, lk1, lq2, lk2, sub_g, lambda_init, b, s):
    t = min(L0_TILE, s)
    it, jt = _causal_pairs(s // t)
    const = lambda bi, p, it, jt: (0, 0)
    vec = lambda a: a.astype(F32).reshape(1, -1)
    n_maps = 2 * DIFF_HEADS
    grid_spec = pltpu.PrefetchScalarGridSpec(
        num_scalar_prefetch=2,
        grid=(b, it.shape[0]),
        in_specs=[pl.BlockSpec((1, 512, t), _FM_Q), pl.BlockSpec((1, t, 512), _TOK_K),
                  pl.BlockSpec((1, 512, t), _FM_K),
                  pl.BlockSpec((1, HEAD_DIM), const), pl.BlockSpec((1, HEAD_DIM), const),
                  pl.BlockSpec((1, HEAD_DIM), const), pl.BlockSpec((1, HEAD_DIM), const),
                  pl.BlockSpec((1, LANES), const)],
        out_specs=pl.BlockSpec((1, t, 512), _TOK_Q),
        scratch_shapes=[pltpu.VMEM((n_maps, LANES, t), BF16),
                        pltpu.VMEM((n_maps, 1, t), F32),
                        pltpu.VMEM((n_maps, 1, t), F32),
                        pltpu.VMEM((n_maps, LANES, t), F32),
                        pltpu.VMEM((n_maps, LANES + BF16_ROWS, t), F32)])
    return pl.pallas_call(
        functools.partial(_diff_attn_kernel, lambda_init),
        grid_spec=grid_spec,
        out_shape=jax.ShapeDtypeStruct((b, s, 512), BF16),
        compiler_params=_params(("arbitrary", "arbitrary")),
        name="diff_attn",
    )(it, jt, aqt, ak.reshape(b, s, 512), avt, vec(lq1), vec(lk1), vec(lq2), vec(lk2),
      vec(sub_g))


def _fox_attn_kernel(it_ref, jt_ref, qt_ref, k_ref, vt_ref, ck_ref, cq_ref, ctk_ref, ctq_ref,
                     o_ref, qz_ref, m_ref, l_ref, acc_ref, pv_ref):
    i, j = it_ref[pl.program_id(1)], jt_ref[pl.program_id(1)]
    tq, tk = qt_ref.shape[2], k_ref.shape[1]

    def values(hd):
        return vt_ref[0, hd * HEAD_DIM:(hd + 1) * HEAD_DIM, :]

    def make_logits(causal):
        decay = (cq_ref[0, 0:1, :] - ck_ref[0]) * LOG2E

        def logits(hd):
            st = _dot(k_ref[0, :, (hd // 2) * LANES:(hd // 2 + 1) * LANES], qz_ref[hd])
            st = st + decay[:, hd:hd + 1]
            return st if causal is None else jnp.where(causal, st, NEG)
        return logits

    @pl.when(j == i)
    def _():
        _fill_masked_queries(qt_ref, qz_ref, FOX_HEADS)
        _init_state(m_ref, l_ref, acc_ref)
        logits = make_logits(_causal_tile(tk, tq, 0, 0))
        _attend_heads(FOX_HEADS, logits, values, m_ref, l_ref, acc_ref)

    @pl.when(j < i)
    def _():
        w = jnp.exp2((ctq_ref[0, :, 0:1] - ctk_ref[0]) * LOG2E)

        def raw_logits(hd):
            return _dot(k_ref[0, :, (hd // 2) * LANES:(hd // 2 + 1) * LANES], qz_ref[hd])

        def weighted_values(hd):
            w_row = w[hd:hd + 1]
            return jnp.concatenate([values(hd) * w_row.astype(BF16), _weight_rows(w_row)], axis=0)

        _attend_heads_lazy(FOX_HEADS, raw_logits, weighted_values, None, make_logits(None),
                           values, m_ref, l_ref, acc_ref, pv_ref)

    @pl.when(j == 0)
    def _():
        for grp in range(FOX_HEADS // 2):
            o = jnp.concatenate([acc_ref[2 * grp] / l_ref[2 * grp],
                                 acc_ref[2 * grp + 1] / l_ref[2 * grp + 1]], axis=0)
            o_ref[0, :, grp * LANES:(grp + 1) * LANES] = o.T.astype(BF16)


def _fox_attn(fqt, fk, fvt, c, ct, b, s):
    t = min(L0_TILE, s)
    it, jt = _causal_pairs(s // t)
    first_rows = lambda bi, p, it, jt: (bi, it[p] * (t // SUBLANES), 0)
    grid_spec = pltpu.PrefetchScalarGridSpec(
        num_scalar_prefetch=2,
        grid=(b, it.shape[0]),
        in_specs=[pl.BlockSpec((1, 512, t), _FM_Q), pl.BlockSpec((1, t, 512), _TOK_K),
                  pl.BlockSpec((1, 512, t), _FM_K), pl.BlockSpec((1, t, LANES), _TOK_K),
                  pl.BlockSpec((1, SUBLANES, LANES), first_rows),
                  pl.BlockSpec((1, FOX_HEADS, t), _FM_K), pl.BlockSpec((1, FOX_HEADS, t), _FM_Q)],
        out_specs=pl.BlockSpec((1, t, 512), _TOK_Q),
        scratch_shapes=[pltpu.VMEM((FOX_HEADS, LANES, t), BF16),
                        pltpu.VMEM((FOX_HEADS, 1, t), F32),
                        pltpu.VMEM((FOX_HEADS, 1, t), F32),
                        pltpu.VMEM((FOX_HEADS, HEAD_DIM, t), F32),
                        pltpu.VMEM((FOX_HEADS, HEAD_DIM + BF16_ROWS, t), F32)])
    return pl.pallas_call(
        _fox_attn_kernel,
        grid_spec=grid_spec,
        out_shape=jax.ShapeDtypeStruct((b, s, 512), BF16),
        compiler_params=_params(("arbitrary", "arbitrary")),
        name="fox_attn",
    )(it, jt, fqt, fk.reshape(b, s, 512), fvt, c, c, ct, ct)


def _mlp_kernel(n_mix, final, *refs):
    h_ref = refs[0]
    mix_refs = refs[1:1 + n_mix]
    wo_refs = refs[1 + n_mix:1 + 2 * n_mix]
    g_ref, w1_ref, w2_ref, gf_ref, out_ref, h1_ref, hn_ref, acc_ref = refs[1 + 2 * n_mix:]
    f = pl.program_id(1)

    @pl.when(f == 0)
    def _():
        h1 = h_ref[...]
        for mix_ref, wo_ref in zip(mix_refs, wo_refs):
            h1 = h1 + _dot(mix_ref[...], wo_ref[...])
        h1_ref[...] = h1
        hn_ref[...] = _rms(h1, g_ref[...]).astype(BF16)
        acc_ref[...] = jnp.zeros_like(acc_ref)

    u = _dot(hn_ref[...], w1_ref[...])
    a = jnp.square(jnp.maximum(u, 0.0)).astype(BF16)
    acc_ref[...] += _dot(a, w2_ref[...])

    @pl.when(f == pl.num_programs(1) - 1)
    def _():
        y = h1_ref[...] + acc_ref[...]
        if final:
            y = _rms(y, gf_ref[...])
        out_ref[...] = y


def _outproj_mlp(h2d, mixes, w_outs, g, w1, w2, g_final, final, s):
    n, d = h2d.shape
    dff = w1.shape[1]
    tm = min(ROW_TILE, s)
    tf = min(FF_TILE, dff)
    row = lambda i, f: (i, 0)
    const = lambda i, f: (0, 0)
    in_specs = [pl.BlockSpec((tm, d), row)]
    in_specs += [pl.BlockSpec((tm, m.shape[1]), row) for m in mixes]
    in_specs += [pl.BlockSpec(w.shape, const) for w in w_outs]
    in_specs += [pl.BlockSpec((1, d), const),
                 pl.BlockSpec((d, tf), lambda i, f: (0, f)),
                 pl.BlockSpec((tf, d), lambda i, f: (f, 0)),
                 pl.BlockSpec((1, d), const)]
    return pl.pallas_call(
        functools.partial(_mlp_kernel, len(mixes), final),
        grid=(n // tm, dff // tf),
        in_specs=in_specs,
        out_specs=pl.BlockSpec((tm, d), row),
        out_shape=jax.ShapeDtypeStruct((n, d), F32),
        scratch_shapes=[pltpu.VMEM((tm, d), F32), pltpu.VMEM((tm, d), BF16),
                        pltpu.VMEM((tm, d), F32)],
        compiler_params=_params(("arbitrary", "arbitrary")),
        name="outproj_mlp",
    )(h2d, *mixes, *[w.astype(BF16) for w in w_outs], g.reshape(1, d), w1.astype(BF16),
      w2.astype(BF16), g_final.reshape(1, d))


def _odd_proj_kernel(x_ref, g_ref, w_ref, wt_ref, lng_ref, lnb_ref, c_ref, s1_ref, s2_ref,
                     cos_ref, sin_ref, k_ref, ik_ref, qt_ref, vt_ref, iqt_ref, iwt_ref):
    hn = _rms(x_ref[...], g_ref[...]).astype(BF16)
    c, s1, s2 = c_ref[...], s1_ref[...], s2_ref[...]
    k = _dot(hn, w_ref[:, 0:1024])
    for grp in range(DSA_HEADS // 2):
        lo = grp * LANES
        k_ref[:, lo:lo + LANES] = _rope_lanes(k[:, lo:lo + LANES], c, s1, s2).astype(BF16)
    ik = _dot(hn, w_ref[:, 1024:1024 + LANES])
    mu = jnp.mean(ik, axis=1, keepdims=True)
    var = jnp.mean(jnp.square(ik - mu), axis=1, keepdims=True)
    ik = (ik - mu) * lax.rsqrt(var + EPS) * lng_ref[...] + lnb_ref[...]
    ik_ref[...] = _rope_lanes(ik, c, s1, s2).astype(BF16)
    cos_t, sin_t = cos_ref[...], sin_ref[...]
    qt_all = _dot_nt(wt_ref[0:1024, :], hn)
    for hd in range(DSA_HEADS):
        lo = hd * HEAD_DIM
        qt = _rope_rows(qt_all[lo:lo + HEAD_DIM], cos_t, sin_t) * Q_SCALE
        qt_ref[0, lo:lo + HEAD_DIM, :] = qt.astype(BF16)
    vt_ref[0] = _dot_nt(wt_ref[1024:2048, :], hn).astype(BF16)
    iqt_all = _dot_nt(wt_ref[2048:2560 + BF16_ROWS, :], hn)
    for hd in range(IDX_HEADS):
        lo = hd * HEAD_DIM
        iqt = _rope_rows(iqt_all[lo:lo + HEAD_DIM], cos_t, sin_t) * HEAD_DIM ** -0.5
        iqt_ref[0, lo:lo + HEAD_DIM, :] = iqt.astype(BF16)
    iwt_ref[0] = iqt_all[512:512 + IDX_HEADS] * IDX_HEADS ** -0.5


def _odd_proj(h2d, g, w_in, ln_g, ln_b, tabs, b, s):
    n, d = h2d.shape
    tm = min(ROW_TILE, s)
    nb = s // tm
    wq, wk, wv, wiq, wik, wiw = (
        w_in[:, 0:1024], w_in[:, 1024:2048], w_in[:, 2048:3072], w_in[:, 3072:3584],
        w_in[:, 3584:3648], w_in[:, 3648:3656])
    wiw = jnp.pad(wiw, ((0, 0), (0, BF16_ROWS - IDX_HEADS)))
    w = jnp.concatenate([wk, wik, wik], axis=1).astype(BF16)
    wt = jnp.concatenate([wq, wv, wiq, wiw], axis=1).T.astype(BF16)
    two = lambda a: jnp.concatenate([a, a]).astype(F32).reshape(1, LANES)
    c, s1, s2, cos_t, sin_t = tabs
    row = lambda i: (i, 0)
    pos = lambda i: (i % nb, 0)
    post = lambda i: (0, i % nb)
    fmaj = lambda i: (i // nb, 0, i % nb)
    const = lambda i: (0, 0)
    return pl.pallas_call(
        _odd_proj_kernel,
        grid=(n // tm,),
        in_specs=[
            pl.BlockSpec((tm, d), row), pl.BlockSpec((1, d), const),
            pl.BlockSpec(w.shape, const), pl.BlockSpec(wt.shape, const),
            pl.BlockSpec((1, LANES), const), pl.BlockSpec((1, LANES), const),
            pl.BlockSpec((tm, LANES), pos), pl.BlockSpec((tm, LANES), pos),
            pl.BlockSpec((tm, LANES), pos),
            pl.BlockSpec((ROT_HALF, tm), post), pl.BlockSpec((ROT_HALF, tm), post)],
        out_specs=(
            pl.BlockSpec((tm, 1024), row), pl.BlockSpec((tm, LANES), row),
            pl.BlockSpec((1, 1024, tm), fmaj), pl.BlockSpec((1, 1024, tm), fmaj),
            pl.BlockSpec((1, 512, tm), fmaj), pl.BlockSpec((1, IDX_HEADS, tm), fmaj)),
        out_shape=(
            jax.ShapeDtypeStruct((n, 1024), BF16), jax.ShapeDtypeStruct((n, LANES), BF16),
            jax.ShapeDtypeStruct((b, 1024, s), BF16), jax.ShapeDtypeStruct((b, 1024, s), BF16),
            jax.ShapeDtypeStruct((b, 512, s), BF16), jax.ShapeDtypeStruct((b, IDX_HEADS, s), F32)),
        compiler_params=_params(("arbitrary",)),
        name="odd_proj",
    )(h2d, g.reshape(1, d), w, wt, two(ln_g), two(ln_b), c, s1, s2, cos_t, sin_t)


def _dsa_attn_kernel(k_sel, it_ref, jt_ref, qt_ref, k_ref, vt_ref, iqt_ref, ik_ref, iwt_ref,
                     o_ref, score_ref, coarse_ref, thr_ref, need_ref, tie_ref, flag_ref, bias_ref,
                     keep_ref, qz_ref, iqz_ref, m_ref, l_ref, acc_ref, pv_ref):
    i, j = it_ref[pl.program_id(1)], jt_ref[pl.program_id(1)]
    tq, tk = qt_ref.shape[2], k_ref.shape[1]

    def count_ge(cand):
        def body(c, part):
            return part + _fold_rows(jnp.where(score_ref[c] >= cand, 1, 0), jnp.add)
        part = lax.fori_loop(0, i + 1, body, jnp.zeros((SUBLANES, tq), I32))
        return jnp.sum(part, axis=0, keepdims=True)

    def count_ge_coarse(cand):
        def body(c, part):
            hit = jnp.where(coarse_ref[c] >= cand, jnp.ones((), BF16), jnp.zeros((), BF16))
            return part + _fold_rows(hit, jnp.add, BF16_ROWS).astype(F32)
        part = lax.fori_loop(0, i + 1, body, jnp.zeros((BF16_ROWS, tq), F32))
        return jnp.sum(part, axis=0, keepdims=True).astype(I32)

    def ordinal_to_float(o):
        x = lax.bitcast_convert_type(o ^ ((o >> 31) & 0x7FFFFFFF), F32)
        return jnp.where(o < NEG_INF_ORDINAL, -jnp.inf, x)

    @pl.when(j == i)
    def _():
        _fill_masked_queries(qt_ref, qz_ref, DSA_HEADS)
        _fill_masked_queries(iqt_ref, iqz_ref, IDX_HEADS)
        _init_state(m_ref, l_ref, acc_ref)
        iw = iwt_ref[0]

        def score_tile(c, carry):
            ik = ik_ref[0, pl.ds(pl.multiple_of(c * tk, tk), tk), :]
            score = jnp.zeros((tk, tq), F32)
            for hd in range(IDX_HEADS):
                score = score + jnp.maximum(_dot(ik, iqz_ref[hd]), 0.0) * iw[hd:hd + 1, :]
            score = jnp.where(score == 0.0, 0.0, score)
            score = jnp.where(_causal_tile(tk, tq, c * tk, i * tq), score, -jnp.inf)
            score_ref[c] = score
            coarse_ref[c] = score.astype(BF16)
            return carry
        lax.fori_loop(0, i + 1, score_tile, 0)

        def coarse_step(step, o):
            cand = o + lax.shift_left(jnp.int32(1), 15 - step)
            cand_f = ordinal_to_float(lax.shift_left(cand, 16)).astype(BF16)
            return jnp.where(count_ge_coarse(cand_f) >= k_sel, cand, o)
        coarse = lax.fori_loop(0, 16, coarse_step, jnp.full((1, tq), -(2 ** 15), I32))
        few = coarse <= (NEG_INF_ORDINAL >> 16)
        base = lax.shift_left(jnp.where(few, 0, coarse) - 1, 16)

        def fine_step(step, u):
            cand = u + lax.shift_left(jnp.int32(1), 16 - step)
            return jnp.where(count_ge(ordinal_to_float(base + cand)) >= k_sel, cand, u)
        fine = lax.fori_loop(0, 17, fine_step, jnp.zeros((1, tq), I32))
        o = jnp.where(few, NEG_INF_ORDINAL, base + fine)
        thr = ordinal_to_float(o)
        n_ge = count_ge(thr)
        n_gt = count_ge(ordinal_to_float(o + 1))
        real = thr > -jnp.inf
        thr_ref[...] = thr
        need_ref[...] = jnp.where(real, k_sel - n_gt, 0).astype(F32)
        flag_ref[0] = jnp.max(jnp.where(real & (n_ge > k_sel), 1, 0))

        @pl.when(flag_ref[0] != 0)
        def _():
            def tie_tile(c, seen):
                tie_ref[c] = seen
                hit = (score_ref[c] == thr) & real
                return seen + jnp.sum(jnp.where(hit, 1.0, 0.0), axis=0, keepdims=True)
            lax.fori_loop(0, i + 1, tie_tile, jnp.zeros((1, tq), F32))

    thr = thr_ref[...]
    kc = score_ref[j]

    def store_selection(sel):
        bias_ref[...] = jnp.where(sel, 0.0, NEG)
        keep_ref[...] = jnp.where(sel, 1.0, 0.0).astype(BF16)

    @pl.when(flag_ref[0] == 0)
    def _():
        store_selection(kc >= jnp.maximum(thr, jnp.finfo(F32).min))

    @pl.when(flag_ref[0] != 0)
    def _():
        gt = kc > thr
        eq = (kc == thr) & (thr > -jnp.inf)
        earlier = (lax.broadcasted_iota(I32, (tk, tk), 1) < lax.broadcasted_iota(I32, (tk, tk), 0))
        eq_f = jnp.where(eq, 1.0, 0.0).astype(BF16)
        rank = _dot(jnp.where(earlier, 1.0, 0.0).astype(BF16), eq_f) + tie_ref[j]
        store_selection(gt | (eq & (rank < need_ref[...])))

    def raw_logits(hd):
        return _dot(k_ref[0, :, (hd // 2) * LANES:(hd // 2 + 1) * LANES], qz_ref[hd])

    def logits(hd):
        return raw_logits(hd) + bias_ref[...]

    def values(hd):
        return vt_ref[0, hd * HEAD_DIM:(hd + 1) * HEAD_DIM, :]

    @pl.when(j == i)
    def _():
        _attend_heads(DSA_HEADS, logits, values, m_ref, l_ref, acc_ref)

    started = jnp.min(m_ref[0]) > 0.5 * NEG

    @pl.when((j < i) & started)
    def _():
        ones = _weight_rows(jnp.ones((1, tk), F32))

        def weighted_values(hd):
            return jnp.concatenate([values(hd), ones], axis=0)

        _attend_heads_lazy(DSA_HEADS, raw_logits, weighted_values, keep_ref[...], logits, values,
                           m_ref, l_ref, acc_ref, pv_ref)

    @pl.when((j < i) & jnp.logical_not(started))
    def _():
        _attend_heads(DSA_HEADS, logits, values, m_ref, l_ref, acc_ref)

    @pl.when(j == 0)
    def _():
        for grp in range(DSA_HEADS // 2):
            o = jnp.concatenate([acc_ref[2 * grp] / l_ref[2 * grp],
                                 acc_ref[2 * grp + 1] / l_ref[2 * grp + 1]], axis=0)
            o_ref[0, :, grp * LANES:(grp + 1) * LANES] = o.T.astype(BF16)


def _dsa_attn(qt, k, vt, iqt, ik, iwt, b, s):
    t = min(DSA_TILE, s)
    nt = s // t
    it, jt = _causal_pairs(nt)
    k_sel = min(TOPK_MAX, s // 4)
    assert t >= k_sel
    tq = tk = t
    grid_spec = pltpu.PrefetchScalarGridSpec(
        num_scalar_prefetch=2,
        grid=(b, it.shape[0]),
        in_specs=[pl.BlockSpec((1, 1024, t), _FM_Q), pl.BlockSpec((1, t, 1024), _TOK_K),
                  pl.BlockSpec((1, 1024, t), _FM_K), pl.BlockSpec((1, 512, t), _FM_Q),
                  pl.BlockSpec((1, s, LANES), lambda bi, p, it, jt: (bi, 0, 0)),
                  pl.BlockSpec((1, IDX_HEADS, t), _FM_Q)],
        out_specs=pl.BlockSpec((1, t, 1024), _TOK_Q),
        scratch_shapes=[pltpu.VMEM((nt, tk, tq), F32),
                        pltpu.VMEM((nt, tk, tq), BF16),
                        pltpu.VMEM((1, tq), F32),
                        pltpu.VMEM((1, tq), F32),
                        pltpu.VMEM((nt, 1, tq), F32),
                        pltpu.SMEM((1,), I32),
                        pltpu.VMEM((tk, tq), F32),
                        pltpu.VMEM((tk, tq), BF16),
                        pltpu.VMEM((DSA_HEADS, LANES, tq), BF16),
                        pltpu.VMEM((IDX_HEADS, LANES, tq), BF16),
                        pltpu.VMEM((DSA_HEADS, 1, tq), F32),
                        pltpu.VMEM((DSA_HEADS, 1, tq), F32),
                        pltpu.VMEM((DSA_HEADS, HEAD_DIM, tq), F32),
                        pltpu.VMEM((DSA_HEADS, HEAD_DIM + BF16_ROWS, tq), F32)])
    return pl.pallas_call(
        functools.partial(_dsa_attn_kernel, k_sel),
        grid_spec=grid_spec,
        out_shape=jax.ShapeDtypeStruct((b, s, 1024), BF16),
        compiler_params=_params(("arbitrary", "arbitrary")),
        name="dsa_attn",
    )(it, jt, qt, k.reshape(b, s, 1024), vt, iqt, ik.reshape(b, s, LANES), iwt)


def _rope_tables(s):
    pos = jnp.arange(s, dtype=F32)
    inv_freq = ROPE_THETA ** (-jnp.arange(0, ROT_DIM, 2, dtype=F32) / ROT_DIM)
    ang = pos[:, None] * inv_freq[None, :]
    cos, sin = jnp.cos(ang), jnp.sin(ang)
    zeros = jnp.zeros((s, HEAD_DIM - ROT_DIM), F32)
    zero8 = jnp.zeros((s, ROT_HALF), F32)
    c = jnp.concatenate([cos, cos, zeros + 1.0], axis=1)
    s1 = jnp.concatenate([zero8, sin, zeros], axis=1)
    s2 = jnp.concatenate([-sin, zero8, zeros], axis=1)
    two = lambda a: jnp.concatenate([a, a], axis=1)
    return two(c), two(s1), two(s2), cos.T, sin.T


def kernel(x, norm_mix, w_in_even, b_forget, lambda_q1, lambda_k1, lambda_q2, lambda_k2,
           diff_subln_g, w_out_even, w_in_odd, idx_ln_g, idx_ln_b, w_out_odd, norm_mlp,
           w_mlp_in, w_mlp_out, norm_final):
    b, s, d = x.shape
    tabs = _rope_tables(s)
    h = x.reshape(b * s, d)

    lambda_init = 0.8 - 0.6 * math.exp(-0.3 * 0)
    ak, fk, fl, aqt, avt, fqt, fvt = _even_proj(h, norm_mix[0], w_in_even[0], tabs, b, s)
    c, ct = _fox_cumsum(fl, b_forget[0], b, s)
    oa = _diff_attn(aqt, ak, avt, lambda_q1[0], lambda_k1[0], lambda_q2[0], lambda_k2[0],
                    diff_subln_g[0], lambda_init, b, s)
    ob = _fox_attn(fqt, fk, fvt, c, ct, b, s)
    h = _outproj_mlp(h, [oa.reshape(b * s, 512), ob.reshape(b * s, 512)],
                     [w_out_even[0][0:512], w_out_even[0][512:1024]], norm_mlp[0],
                     w_mlp_in[0], w_mlp_out[0], norm_final, False, s)

    k, ik, qt, vt, iqt, iwt = _odd_proj(h, norm_mix[1], w_in_odd[0], idx_ln_g[0], idx_ln_b[0],
                                        tabs, b, s)
    o = _dsa_attn(qt, k, vt, iqt, ik, iwt, b, s)
    h = _outproj_mlp(h, [o.reshape(b * s, 1024)], [w_out_odd[0]], norm_mlp[1],
                     w_mlp_in[1], w_mlp_out[1], norm_final, True, s)
    return h.reshape(b, s, d)
```

```python
import functools
import math
from typing import Any, NamedTuple

import jax
import jax.numpy as jnp
from jax import lax
from jax.experimental import pallas as pl
from jax.experimental.pallas import tpu as pltpu

F32 = jnp.float32
BF16 = jnp.bfloat16
I32 = jnp.int32

HEAD_DIM = 64
ROT_DIM = HEAD_DIM // 4
ROT_HALF = ROT_DIM // 2
ROPE_THETA = 500000.0
EPS = 1e-6
DIFF_HEADS = 4
FOX_HEADS = 8
DSA_HEADS = 16
IDX_HEADS = 8
TOPK_MAX = 256

LANES = 128
SUBLANES = 8
BF16_ROWS = 16
NEG = -1e30
MAX_SLACK = 16.0
INT_MIN = -(2 ** 31)
NEG_INF_ORDINAL = INT_MIN + 0x7FFFFF
LOG2E = math.log2(math.e)
Q_SCALE = HEAD_DIM ** -0.5 * LOG2E
VMEM_LIMIT = 56 * 1024 * 1024

ROW_TILE = 512
FF_TILE = 1024
CUMSUM_TILE = 256
L0_TILE = 512
DSA_TILE = 512


def _params(sem):
    return pltpu.CompilerParams(dimension_semantics=sem, vmem_limit_bytes=VMEM_LIMIT)


def _rms(x, g):
    return x * lax.rsqrt(jnp.mean(x * x, axis=1, keepdims=True) + EPS) * g


def _dot(a, b):
    return jnp.dot(a, b, preferred_element_type=F32)


def _dot_nt(a, b):
    return lax.dot_general(a, b, (((1,), (1,)), ((), ())), preferred_element_type=F32)


def _rope_lanes(x, c, s1, s2):
    return x * c + pltpu.roll(x, ROT_HALF, 1) * s1 + pltpu.roll(x, LANES - ROT_HALF, 1) * s2


def _rope_rows(xh, cos_t, sin_t):
    x1 = xh[0:ROT_HALF]
    x2 = xh[ROT_HALF:ROT_DIM]
    return jnp.concatenate(
        [x1 * cos_t - x2 * sin_t, x2 * cos_t + x1 * sin_t, xh[ROT_DIM:]], axis=0)


def _even_proj_kernel(x_ref, g_ref, w_ref, wt_ref, c_ref, s1_ref, s2_ref, cos_ref, sin_ref,
                      ak_ref, fk_ref, fl_ref, aqt_ref, avt_ref, fqt_ref, fvt_ref):
    hn = _rms(x_ref[...], g_ref[...]).astype(BF16)
    c, s1, s2 = c_ref[...], s1_ref[...], s2_ref[...]
    ak = _dot(hn, w_ref[:, 0:512])
    for grp in range(DIFF_HEADS):
        lo = grp * LANES
        ak_ref[:, lo:lo + LANES] = _rope_lanes(ak[:, lo:lo + LANES], c, s1, s2).astype(BF16)
    fk_ref[...] = _dot(hn, w_ref[:, 512:1024]).astype(BF16)
    fl_ref[...] = _dot(hn, w_ref[:, 1024:1024 + LANES])
    cos_t, sin_t = cos_ref[...], sin_ref[...]
    aqt = _dot_nt(wt_ref[0:512, :], hn)
    for hd in range(2 * DIFF_HEADS):
        lo = hd * HEAD_DIM
        qt = _rope_rows(aqt[lo:lo + HEAD_DIM], cos_t, sin_t) * Q_SCALE
        aqt_ref[0, lo:lo + HEAD_DIM, :] = qt.astype(BF16)
    avt_ref[0] = _dot_nt(wt_ref[512:1024, :], hn).astype(BF16)
    fqt_ref[0] = (_dot_nt(wt_ref[1024:1536, :], hn) * Q_SCALE).astype(BF16)
    fvt_ref[0] = _dot_nt(wt_ref[1536:2048, :], hn).astype(BF16)


def _even_proj(h2d, g, w_in, tabs, b, s):
    n, d = h2d.shape
    tm = min(ROW_TILE, s)
    nb = s // tm
    a_q, a_k, a_v, f_q, f_k, f_v, f_l = (
        w_in[:, 0:512], w_in[:, 512:1024], w_in[:, 1024:1536], w_in[:, 1536:2048],
        w_in[:, 2048:2560], w_in[:, 2560:3072], w_in[:, 3072:3080])
    f_l = jnp.pad(f_l, ((0, 0), (0, LANES - FOX_HEADS)))
    w = jnp.concatenate([a_k, f_k, f_l], axis=1).astype(BF16)
    wt = jnp.concatenate([a_q, a_v, f_q, f_v], axis=1).T.astype(BF16)
    c, s1, s2, cos_t, sin_t = tabs
    row = lambda i: (i, 0)
    pos = lambda i: (i % nb, 0)
    post = lambda i: (0, i % nb)
    fmaj = lambda i: (i // nb, 0, i % nb)
    const = lambda i: (0, 0)
    fm_shape = jax.ShapeDtypeStruct((b, 512, s), BF16)
    fm_spec = pl.BlockSpec((1, 512, tm), fmaj)
    return pl.pallas_call(
        _even_proj_kernel,
        grid=(n // tm,),
        in_specs=[
            pl.BlockSpec((tm, d), row), pl.BlockSpec((1, d), const),
            pl.BlockSpec(w.shape, const), pl.BlockSpec(wt.shape, const),
            pl.BlockSpec((tm, LANES), pos), pl.BlockSpec((tm, LANES), pos),
            pl.BlockSpec((tm, LANES), pos),
            pl.BlockSpec((ROT_HALF, tm), post), pl.BlockSpec((ROT_HALF, tm), post)],
        out_specs=(
            pl.BlockSpec((tm, 512), row), pl.BlockSpec((tm, 512), row),
            pl.BlockSpec((tm, LANES), row), fm_spec, fm_spec, fm_spec, fm_spec),
        out_shape=(
            jax.ShapeDtypeStruct((n, 512), BF16), jax.ShapeDtypeStruct((n, 512), BF16),
            jax.ShapeDtypeStruct((n, LANES), F32), fm_shape, fm_shape, fm_shape, fm_shape),
        compiler_params=_params(("arbitrary",)),
        name="even_proj",
    )(h2d, g.reshape(1, d), w, wt, c, s1, s2, cos_t, sin_t)


def _fox_cumsum_kernel(fl_ref, bf_ref, c_ref, ct_ref, carry_ref):
    @pl.when(pl.program_id(1) == 0)
    def _():
        carry_ref[...] = jnp.zeros_like(carry_ref)

    z = fl_ref[0] + bf_ref[...]
    logf = jnp.minimum(z, 0.0) - jnp.log1p(jnp.exp(-jnp.abs(z)))
    t = z.shape[0]
    tri = (lax.broadcasted_iota(I32, (t, t), 0) >= lax.broadcasted_iota(I32, (t, t), 1)).astype(F32)
    cs = jnp.dot(tri, logf, precision=lax.Precision.HIGHEST,
                 preferred_element_type=F32) + carry_ref[...]
    c_ref[0] = cs
    ct_ref[0] = cs.T[0:FOX_HEADS, :]
    carry_ref[...] = cs[t - 1:t, :]


def _fox_cumsum(fl, b_f, b, s):
    tc = min(CUMSUM_TILE, s)
    bf = jnp.pad(b_f.astype(F32), (0, LANES - FOX_HEADS)).reshape(1, LANES)
    return pl.pallas_call(
        _fox_cumsum_kernel,
        grid=(b, s // tc),
        in_specs=[pl.BlockSpec((1, tc, LANES), lambda bi, i: (bi, i, 0)),
                  pl.BlockSpec((1, LANES), lambda bi, i: (0, 0))],
        out_specs=(pl.BlockSpec((1, tc, LANES), lambda bi, i: (bi, i, 0)),
                   pl.BlockSpec((1, FOX_HEADS, tc), lambda bi, i: (bi, 0, i))),
        out_shape=(jax.ShapeDtypeStruct((b, s, LANES), F32),
                   jax.ShapeDtypeStruct((b, FOX_HEADS, s), F32)),
        scratch_shapes=[pltpu.VMEM((1, LANES), F32)],
        compiler_params=_params(("arbitrary", "arbitrary")),
        name="fox_cumsum",
    )(fl.reshape(b, s, LANES), bf)


def _fill_masked_queries(qt_ref, qz_ref, n_heads):
    row = lax.broadcasted_iota(I32, (LANES, qt_ref.shape[2]), 0)
    for hd in range(n_heads):
        grp = hd // 2
        qg = qt_ref[0, grp * LANES:(grp + 1) * LANES, :]
        keep = (row < HEAD_DIM) if hd % 2 == 0 else (row >= HEAD_DIM)
        qz_ref[hd] = jnp.where(keep, qg, jnp.zeros_like(qg))


def _init_state(m_ref, l_ref, acc_ref):
    m_ref[...] = jnp.full(m_ref.shape, NEG, F32)
    l_ref[...] = jnp.zeros_like(l_ref)
    acc_ref[...] = jnp.zeros_like(acc_ref)


def _fold_rows(x, op, group=SUBLANES):
    r, t = x.shape
    x = x.reshape(r // group, group, t)
    n = r // group
    while n > 1:
        assert n % 2 == 0
        n //= 2
        x = op(x[:n], x[n:])
    return x[0]


def _reduce_rows(x, op):
    x = _fold_rows(x, op)
    n = SUBLANES
    while n > 1:
        n //= 2
        x = op(x[:n], x[n:])
    return x


class _HeadState(NamedTuple):
    m: Any
    l: Any
    acc: Any
    pv: Any


def _head_states(m_ref, l_ref, acc_ref, pv_ref):
    return [_HeadState(m_ref.at[h], l_ref.at[h], acc_ref.at[h], pv_ref.at[h])
            for h in range(m_ref.shape[0])]


def _softmax_update(st, head):
    m_prev = head.m[...]
    m_next = jnp.maximum(m_prev, _reduce_rows(st, jnp.maximum))
    alpha = jnp.exp2(m_prev - m_next)
    p = jnp.exp2(st - m_next)
    head.l[...] = alpha * head.l[...] + _reduce_rows(p, jnp.add)
    head.m[...] = m_next
    return alpha, p.astype(BF16)


def _attend_heads(heads, logits_fn, value_fn):
    st = logits_fn(0)
    for hd, head in enumerate(heads):
        st_next = logits_fn(hd + 1) if hd + 1 < len(heads) else None
        alpha, p = _softmax_update(st, head)
        head.acc[...] = alpha * head.acc[...] + _dot(value_fn(hd), p)
        st = st_next


def _weight_rows(w_row):
    rows = lax.broadcasted_iota(I32, (BF16_ROWS, w_row.shape[1]), 0)
    return jnp.where(rows == 0, w_row, 0.0).astype(BF16)


def _attend_heads_lazy(heads, raw_logits_fn, weighted_value_fn, keep, logits_fn, value_fn):
    excess = jnp.full(heads[0].m.shape, NEG, F32)
    st = raw_logits_fn(0)
    for hd, head in enumerate(heads):
        st_next = raw_logits_fn(hd + 1) if hd + 1 < len(heads) else None
        m_prev = head.m[...]
        excess = jnp.maximum(excess, _reduce_rows(st, jnp.maximum) - m_prev)
        p = jnp.exp2(st - m_prev).astype(BF16)
        head.pv[...] = _dot(weighted_value_fn(hd), p if keep is None else p * keep)
        st = st_next
    within = jnp.max(excess) <= MAX_SLACK

    @pl.when(within)
    def _():
        for head in heads:
            d = head.acc.shape[0]
            head.acc[...] = head.acc[...] + head.pv[0:d]
            head.l[...] = head.l[...] + head.pv[d:d + 1]

    @pl.when(jnp.logical_not(within))
    def _():
        _attend_heads(heads, logits_fn, value_fn)


def _causal_pairs(nt):
    it = [i for i in range(nt) for _ in range(i + 1)]
    jt = [j for i in range(nt) for j in range(i, -1, -1)]
    return jnp.asarray(it, I32), jnp.asarray(jt, I32)


_TOK_Q = lambda bi, p, it, jt: (bi, it[p], 0)
_FM_Q = lambda bi, p, it, jt: (bi, 0, it[p])
_TOK_K = lambda bi, p, it, jt: (bi, jt[p], 0)
_FM_K = lambda bi, p, it, jt: (bi, 0, jt[p])


def _causal_tile(tk, tq, k0, q0):
    kpos = k0 + lax.broadcasted_iota(I32, (tk, tq), 0)
    qpos = q0 + lax.broadcasted_iota(I32, (tk, tq), 1)
    return kpos <= qpos


def _l0_attn_kernel(lambda_init, it_ref, jt_ref, aqt_ref, ak_ref, avt_ref, fqt_ref, fk_ref,
                    fvt_ref, ck_ref, cq_ref, ctk_ref, ctq_ref, lq1_ref, lk1_ref, lq2_ref,
                    lk2_ref, subg_ref, oa_ref, ob_ref, aqz_ref, am_ref, al_ref, aacc_ref,
                    apv_ref, fqz_ref, fm_ref, fl_ref, facc_ref, fpv_ref):
    i, j = it_ref[pl.program_id(1)], jt_ref[pl.program_id(1)]
    tq, tk = aqt_ref.shape[2], ak_ref.shape[1]
    n_maps = 2 * DIFF_HEADS
    heads = (_head_states(am_ref, al_ref, aacc_ref, apv_ref)
             + _head_states(fm_ref, fl_ref, facc_ref, fpv_ref))

    def raw_logits(mp):
        if mp < n_maps:
            return _dot(ak_ref[0, :, (mp // 2) * LANES:(mp // 2 + 1) * LANES], aqz_ref[mp])
        hd = mp - n_maps
        return _dot(fk_ref[0, :, (hd // 2) * LANES:(hd // 2 + 1) * LANES], fqz_ref[hd])

    def values(mp):
        if mp < n_maps:
            return avt_ref[0, (mp // 2) * LANES:(mp // 2 + 1) * LANES, :]
        hd = mp - n_maps
        return fvt_ref[0, hd * HEAD_DIM:(hd + 1) * HEAD_DIM, :]

    def make_logits(causal):
        decay = (cq_ref[0, 0:1, :] - ck_ref[0]) * LOG2E

        def logits(mp):
            st = raw_logits(mp)
            if mp >= n_maps:
                st = st + decay[:, mp - n_maps:mp - n_maps + 1]
            return st if causal is None else jnp.where(causal, st, NEG)
        return logits

    @pl.when(j == i)
    def _():
        _fill_masked_queries(aqt_ref, aqz_ref, n_maps)
        _fill_masked_queries(fqt_ref, fqz_ref, FOX_HEADS)
        _init_state(am_ref, al_ref, aacc_ref)
        _init_state(fm_ref, fl_ref, facc_ref)
        _attend_heads(heads, make_logits(_causal_tile(tk, tq, 0, 0)), values)

    @pl.when(j < i)
    def _():
        w = jnp.exp2((ctq_ref[0, :, 0:1] - ctk_ref[0]) * LOG2E)
        ones = _weight_rows(jnp.ones((1, tk), F32))

        def weighted_values(mp):
            if mp < n_maps:
                return jnp.concatenate([values(mp), ones], axis=0)
            w_row = w[mp - n_maps:mp - n_maps + 1]
            return jnp.concatenate([values(mp) * w_row.astype(BF16), _weight_rows(w_row)], axis=0)

        _attend_heads_lazy(heads, raw_logits, weighted_values, None, make_logits(None), values)

    @pl.when(j == 0)
    def _():
        lam = (jnp.exp(jnp.sum(lq1_ref[...] * lk1_ref[...], axis=1, keepdims=True))
               - jnp.exp(jnp.sum(lq2_ref[...] * lk2_ref[...], axis=1, keepdims=True))
               + lambda_init)
        for hd in range(DIFF_HEADS):
            o = (aacc_ref[2 * hd] / al_ref[2 * hd]
                 - lam * (aacc_ref[2 * hd + 1] / al_ref[2 * hd + 1]))
            o = o * lax.rsqrt(jnp.mean(o * o, axis=0, keepdims=True) + EPS)
            o = o.T * (subg_ref[...] * (1.0 - lambda_init))
            oa_ref[0, :, hd * LANES:(hd + 1) * LANES] = o.astype(BF16)
        for grp in range(FOX_HEADS // 2):
            o = jnp.concatenate([facc_ref[2 * grp] / fl_ref[2 * grp],
                                 facc_ref[2 * grp + 1] / fl_ref[2 * grp + 1]], axis=0)
            ob_ref[0, :, grp * LANES:(grp + 1) * LANES] = o.T.astype(BF16)


def _l0_attn(aqt, ak, avt, fqt, fk, fvt, c, ct, lq1, lk1, lq2, lk2, sub_g, lambda_init, b, s):
    t = min(L0_TILE, s)
    it, jt = _causal_pairs(s // t)
    const = lambda bi, p, it, jt: (0, 0)
    first_rows = lambda bi, p, it, jt: (bi, it[p] * (t // SUBLANES), 0)
    vec = lambda a: a.astype(F32).reshape(1, -1)
    n_maps = 2 * DIFF_HEADS
    qkv_specs = [pl.BlockSpec((1, 512, t), _FM_Q), pl.BlockSpec((1, t, 512), _TOK_K),
                 pl.BlockSpec((1, 512, t), _FM_K)]
    grid_spec = pltpu.PrefetchScalarGridSpec(
        num_scalar_prefetch=2,
        grid=(b, it.shape[0]),
        in_specs=qkv_specs + qkv_specs + [
            pl.BlockSpec((1, t, LANES), _TOK_K), pl.BlockSpec((1, SUBLANES, LANES), first_rows),
            pl.BlockSpec((1, FOX_HEADS, t), _FM_K), pl.BlockSpec((1, FOX_HEADS, t), _FM_Q),
            pl.BlockSpec((1, HEAD_DIM), const), pl.BlockSpec((1, HEAD_DIM), const),
            pl.BlockSpec((1, HEAD_DIM), const), pl.BlockSpec((1, HEAD_DIM), const),
            pl.BlockSpec((1, LANES), const)],
        out_specs=(pl.BlockSpec((1, t, 512), _TOK_Q), pl.BlockSpec((1, t, 512), _TOK_Q)),
        scratch_shapes=[pltpu.VMEM((n_maps, LANES, t), BF16),
                        pltpu.VMEM((n_maps, 1, t), F32),
                        pltpu.VMEM((n_maps, 1, t), F32),
                        pltpu.VMEM((n_maps, LANES, t), F32),
                        pltpu.VMEM((n_maps, LANES + BF16_ROWS, t), F32),
                        pltpu.VMEM((FOX_HEADS, LANES, t), BF16),
                        pltpu.VMEM((FOX_HEADS, 1, t), F32),
                        pltpu.VMEM((FOX_HEADS, 1, t), F32),
                        pltpu.VMEM((FOX_HEADS, HEAD_DIM, t), F32),
                        pltpu.VMEM((FOX_HEADS, HEAD_DIM + BF16_ROWS, t), F32)])
    return pl.pallas_call(
        functools.partial(_l0_attn_kernel, lambda_init),
        grid_spec=grid_spec,
        out_shape=(jax.ShapeDtypeStruct((b, s, 512), BF16),
                   jax.ShapeDtypeStruct((b, s, 512), BF16)),
        compiler_params=_params(("arbitrary", "arbitrary")),
        name="l0_attn",
    )(it, jt, aqt, ak.reshape(b, s, 512), avt, fqt, fk.reshape(b, s, 512), fvt, c, c, ct, ct,
      vec(lq1), vec(lk1), vec(lq2), vec(lk2), vec(sub_g))


def _mlp_kernel(n_mix, final, *refs):
    h_ref = refs[0]
    mix_refs = refs[1:1 + n_mix]
    wo_refs = refs[1 + n_mix:1 + 2 * n_mix]
    g_ref, w1_ref, w2_ref, gf_ref, out_ref, h1_ref, hn_ref, acc_ref = refs[1 + 2 * n_mix:]
    f = pl.program_id(1)

    @pl.when(f == 0)
    def _():
        h1 = h_ref[...]
        for mix_ref, wo_ref in zip(mix_refs, wo_refs):
            h1 = h1 + _dot(mix_ref[...], wo_ref[...])
        h1_ref[...] = h1
        hn_ref[...] = _rms(h1, g_ref[...]).astype(BF16)
        acc_ref[...] = jnp.zeros_like(acc_ref)

    u = _dot(hn_ref[...], w1_ref[...])
    a = jnp.square(jnp.maximum(u, 0.0)).astype(BF16)
    acc_ref[...] += _dot(a, w2_ref[...])

    @pl.when(f == pl.num_programs(1) - 1)
    def _():
        y = h1_ref[...] + acc_ref[...]
        if final:
            y = _rms(y, gf_ref[...])
        out_ref[...] = y


def _outproj_mlp(h2d, mixes, w_outs, g, w1, w2, g_final, final, s):
    n, d = h2d.shape
    dff = w1.shape[1]
    tm = min(ROW_TILE, s)
    tf = min(FF_TILE, dff)
    row = lambda i, f: (i, 0)
    const = lambda i, f: (0, 0)
    in_specs = [pl.BlockSpec((tm, d), row)]
    in_specs += [pl.BlockSpec((tm, m.shape[1]), row) for m in mixes]
    in_specs += [pl.BlockSpec(w.shape, const) for w in w_outs]
    in_specs += [pl.BlockSpec((1, d), const),
                 pl.BlockSpec((d, tf), lambda i, f: (0, f)),
                 pl.BlockSpec((tf, d), lambda i, f: (f, 0)),
                 pl.BlockSpec((1, d), const)]
    return pl.pallas_call(
        functools.partial(_mlp_kernel, len(mixes), final),
        grid=(n // tm, dff // tf),
        in_specs=in_specs,
        out_specs=pl.BlockSpec((tm, d), row),
        out_shape=jax.ShapeDtypeStruct((n, d), F32),
        scratch_shapes=[pltpu.VMEM((tm, d), F32), pltpu.VMEM((tm, d), BF16),
                        pltpu.VMEM((tm, d), F32)],
        compiler_params=_params(("arbitrary", "arbitrary")),
        name="outproj_mlp",
    )(h2d, *mixes, *[w.astype(BF16) for w in w_outs], g.reshape(1, d), w1.astype(BF16),
      w2.astype(BF16), g_final.reshape(1, d))


def _odd_proj_kernel(x_ref, g_ref, w_ref, wt_ref, lng_ref, lnb_ref, c_ref, s1_ref, s2_ref,
                     cos_ref, sin_ref, k_ref, ik_ref, qt_ref, vt_ref, iqt_ref, iwt_ref):
    hn = _rms(x_ref[...], g_ref[...]).astype(BF16)
    c, s1, s2 = c_ref[...], s1_ref[...], s2_ref[...]
    k = _dot(hn, w_ref[:, 0:1024])
    for grp in range(DSA_HEADS // 2):
        lo = grp * LANES
        k_ref[:, lo:lo + LANES] = _rope_lanes(k[:, lo:lo + LANES], c, s1, s2).astype(BF16)
    ik = _dot(hn, w_ref[:, 1024:1024 + LANES])
    mu = jnp.mean(ik, axis=1, keepdims=True)
    var = jnp.mean(jnp.square(ik - mu), axis=1, keepdims=True)
    ik = (ik - mu) * lax.rsqrt(var + EPS) * lng_ref[...] + lnb_ref[...]
    ik_ref[...] = _rope_lanes(ik, c, s1, s2).astype(BF16)
    cos_t, sin_t = cos_ref[...], sin_ref[...]
    qt_all = _dot_nt(wt_ref[0:1024, :], hn)
    for hd in range(DSA_HEADS):
        lo = hd * HEAD_DIM
        qt = _rope_rows(qt_all[lo:lo + HEAD_DIM], cos_t, sin_t) * Q_SCALE
        qt_ref[0, lo:lo + HEAD_DIM, :] = qt.astype(BF16)
    vt_ref[0] = _dot_nt(wt_ref[1024:2048, :], hn).astype(BF16)
    iqt_all = _dot_nt(wt_ref[2048:2560 + BF16_ROWS, :], hn)
    for hd in range(IDX_HEADS):
        lo = hd * HEAD_DIM
        iqt = _rope_rows(iqt_all[lo:lo + HEAD_DIM], cos_t, sin_t) * HEAD_DIM ** -0.5
        iqt_ref[0, lo:lo + HEAD_DIM, :] = iqt.astype(BF16)
    iwt_ref[0] = iqt_all[512:512 + IDX_HEADS] * IDX_HEADS ** -0.5


def _odd_proj(h2d, g, w_in, ln_g, ln_b, tabs, b, s):
    n, d = h2d.shape
    tm = min(ROW_TILE, s)
    nb = s // tm
    wq, wk, wv, wiq, wik, wiw = (
        w_in[:, 0:1024], w_in[:, 1024:2048], w_in[:, 2048:3072], w_in[:, 3072:3584],
        w_in[:, 3584:3648], w_in[:, 3648:3656])
    wiw = jnp.pad(wiw, ((0, 0), (0, BF16_ROWS - IDX_HEADS)))
    w = jnp.concatenate([wk, wik, wik], axis=1).astype(BF16)
    wt = jnp.concatenate([wq, wv, wiq, wiw], axis=1).T.astype(BF16)
    two = lambda a: jnp.concatenate([a, a]).astype(F32).reshape(1, LANES)
    c, s1, s2, cos_t, sin_t = tabs
    row = lambda i: (i, 0)
    pos = lambda i: (i % nb, 0)
    post = lambda i: (0, i % nb)
    fmaj = lambda i: (i // nb, 0, i % nb)
    const = lambda i: (0, 0)
    return pl.pallas_call(
        _odd_proj_kernel,
        grid=(n // tm,),
        in_specs=[
            pl.BlockSpec((tm, d), row), pl.BlockSpec((1, d), const),
            pl.BlockSpec(w.shape, const), pl.BlockSpec(wt.shape, const),
            pl.BlockSpec((1, LANES), const), pl.BlockSpec((1, LANES), const),
            pl.BlockSpec((tm, LANES), pos), pl.BlockSpec((tm, LANES), pos),
            pl.BlockSpec((tm, LANES), pos),
            pl.BlockSpec((ROT_HALF, tm), post), pl.BlockSpec((ROT_HALF, tm), post)],
        out_specs=(
            pl.BlockSpec((tm, 1024), row), pl.BlockSpec((tm, LANES), row),
            pl.BlockSpec((1, 1024, tm), fmaj), pl.BlockSpec((1, 1024, tm), fmaj),
            pl.BlockSpec((1, 512, tm), fmaj), pl.BlockSpec((1, IDX_HEADS, tm), fmaj)),
        out_shape=(
            jax.ShapeDtypeStruct((n, 1024), BF16), jax.ShapeDtypeStruct((n, LANES), BF16),
            jax.ShapeDtypeStruct((b, 1024, s), BF16), jax.ShapeDtypeStruct((b, 1024, s), BF16),
            jax.ShapeDtypeStruct((b, 512, s), BF16), jax.ShapeDtypeStruct((b, IDX_HEADS, s), F32)),
        compiler_params=_params(("arbitrary",)),
        name="odd_proj",
    )(h2d, g.reshape(1, d), w, wt, two(ln_g), two(ln_b), c, s1, s2, cos_t, sin_t)


def _dsa_attn_kernel(k_sel, it_ref, jt_ref, qt_ref, k_ref, vt_ref, iqt_ref, ik_ref, iwt_ref,
                     o_ref, score_ref, coarse_ref, thr_ref, need_ref, tie_ref, flag_ref, bias_ref,
                     keep_ref, qz_ref, iqz_ref, m_ref, l_ref, acc_ref, pv_ref):
    i, j = it_ref[pl.program_id(1)], jt_ref[pl.program_id(1)]
    tq, tk = qt_ref.shape[2], k_ref.shape[1]

    def count_ge(cand):
        def body(c, part):
            return part + _fold_rows(jnp.where(score_ref[c] >= cand, 1, 0), jnp.add)
        part = lax.fori_loop(0, i + 1, body, jnp.zeros((SUBLANES, tq), I32))
        return jnp.sum(part, axis=0, keepdims=True)

    def count_ge_coarse(cand):
        def body(c, part):
            hit = jnp.where(coarse_ref[c] >= cand, jnp.ones((), BF16), jnp.zeros((), BF16))
            return part + _fold_rows(hit, jnp.add, BF16_ROWS).astype(F32)
        part = lax.fori_loop(0, i + 1, body, jnp.zeros((BF16_ROWS, tq), F32))
        return jnp.sum(part, axis=0, keepdims=True).astype(I32)

    def ordinal_to_float(o):
        x = lax.bitcast_convert_type(o ^ ((o >> 31) & 0x7FFFFFFF), F32)
        return jnp.where(o < NEG_INF_ORDINAL, -jnp.inf, x)

    @pl.when(j == i)
    def _():
        _fill_masked_queries(qt_ref, qz_ref, DSA_HEADS)
        _fill_masked_queries(iqt_ref, iqz_ref, IDX_HEADS)
        _init_state(m_ref, l_ref, acc_ref)
        iw = iwt_ref[0]

        def score_tile(c, carry):
            ik = ik_ref[0, pl.ds(pl.multiple_of(c * tk, tk), tk), :]
            score = jnp.zeros((tk, tq), F32)
            for hd in range(IDX_HEADS):
                score = score + jnp.maximum(_dot(ik, iqz_ref[hd]), 0.0) * iw[hd:hd + 1, :]
            score = jnp.where(score == 0.0, 0.0, score)
            score = jnp.where(_causal_tile(tk, tq, c * tk, i * tq), score, -jnp.inf)
            score_ref[c] = score
            coarse_ref[c] = score.astype(BF16)
            return carry
        lax.fori_loop(0, i + 1, score_tile, 0)

        def coarse_step(step, o):
            cand = o + lax.shift_left(jnp.int32(1), 15 - step)
            cand_f = ordinal_to_float(lax.shift_left(cand, 16)).astype(BF16)
            return jnp.where(count_ge_coarse(cand_f) >= k_sel, cand, o)
        coarse = lax.fori_loop(0, 16, coarse_step, jnp.full((1, tq), -(2 ** 15), I32))
        few = coarse <= (NEG_INF_ORDINAL >> 16)
        base = lax.shift_left(jnp.where(few, 0, coarse) - 1, 16)

        def fine_step(step, u):
            cand = u + lax.shift_left(jnp.int32(1), 16 - step)
            return jnp.where(count_ge(ordinal_to_float(base + cand)) >= k_sel, cand, u)
        fine = lax.fori_loop(0, 17, fine_step, jnp.zeros((1, tq), I32))
        o = jnp.where(few, NEG_INF_ORDINAL, base + fine)
        thr = ordinal_to_float(o)
        n_ge = count_ge(thr)
        n_gt = count_ge(ordinal_to_float(o + 1))
        real = thr > -jnp.inf
        thr_ref[...] = thr
        need_ref[...] = jnp.where(real, k_sel - n_gt, 0).astype(F32)
        flag_ref[0] = jnp.max(jnp.where(real & (n_ge > k_sel), 1, 0))

        @pl.when(flag_ref[0] != 0)
        def _():
            def tie_tile(c, seen):
                tie_ref[c] = seen
                hit = (score_ref[c] == thr) & real
                return seen + jnp.sum(jnp.where(hit, 1.0, 0.0), axis=0, keepdims=True)
            lax.fori_loop(0, i + 1, tie_tile, jnp.zeros((1, tq), F32))

    thr = thr_ref[...]
    kc = score_ref[j]

    def store_selection(sel):
        bias_ref[...] = jnp.where(sel, 0.0, NEG)
        keep_ref[...] = jnp.where(sel, 1.0, 0.0).astype(BF16)

    @pl.when(flag_ref[0] == 0)
    def _():
        store_selection(kc >= jnp.maximum(thr, jnp.finfo(F32).min))

    @pl.when(flag_ref[0] != 0)
    def _():
        gt = kc > thr
        eq = (kc == thr) & (thr > -jnp.inf)
        earlier = (lax.broadcasted_iota(I32, (tk, tk), 1) < lax.broadcasted_iota(I32, (tk, tk), 0))
        eq_f = jnp.where(eq, 1.0, 0.0).astype(BF16)
        rank = _dot(jnp.where(earlier, 1.0, 0.0).astype(BF16), eq_f) + tie_ref[j]
        store_selection(gt | (eq & (rank < need_ref[...])))

    def raw_logits(hd):
        return _dot(k_ref[0, :, (hd // 2) * LANES:(hd // 2 + 1) * LANES], qz_ref[hd])

    def logits(hd):
        return raw_logits(hd) + bias_ref[...]

    def values(hd):
        return vt_ref[0, hd * HEAD_DIM:(hd + 1) * HEAD_DIM, :]

    heads = _head_states(m_ref, l_ref, acc_ref, pv_ref)

    @pl.when(j == i)
    def _():
        _attend_heads(heads, logits, values)

    started = jnp.min(m_ref[0]) > 0.5 * NEG

    @pl.when((j < i) & started)
    def _():
        ones = _weight_rows(jnp.ones((1, tk), F32))

        def weighted_values(hd):
            return jnp.concatenate([values(hd), ones], axis=0)

        _attend_heads_lazy(heads, raw_logits, weighted_values, keep_ref[...], logits, values)

    @pl.when((j < i) & jnp.logical_not(started))
    def _():
        _attend_heads(heads, logits, values)

    @pl.when(j == 0)
    def _():
        for grp in range(DSA_HEADS // 2):
            o = jnp.concatenate([acc_ref[2 * grp] / l_ref[2 * grp],
                                 acc_ref[2 * grp + 1] / l_ref[2 * grp + 1]], axis=0)
            o_ref[0, :, grp * LANES:(grp + 1) * LANES] = o.T.astype(BF16)


def _dsa_attn(qt, k, vt, iqt, ik, iwt, b, s):
    t = min(DSA_TILE, s)
    nt = s // t
    it, jt = _causal_pairs(nt)
    k_sel = min(TOPK_MAX, s // 4)
    assert t >= k_sel
    tq = tk = t
    grid_spec = pltpu.PrefetchScalarGridSpec(
        num_scalar_prefetch=2,
        grid=(b, it.shape[0]),
        in_specs=[pl.BlockSpec((1, 1024, t), _FM_Q), pl.BlockSpec((1, t, 1024), _TOK_K),
                  pl.BlockSpec((1, 1024, t), _FM_K), pl.BlockSpec((1, 512, t), _FM_Q),
                  pl.BlockSpec((1, s, LANES), lambda bi, p, it, jt: (bi, 0, 0)),
                  pl.BlockSpec((1, IDX_HEADS, t), _FM_Q)],
        out_specs=pl.BlockSpec((1, t, 1024), _TOK_Q),
        scratch_shapes=[pltpu.VMEM((nt, tk, tq), F32),
                        pltpu.VMEM((nt, tk, tq), BF16),
                        pltpu.VMEM((1, tq), F32),
                        pltpu.VMEM((1, tq), F32),
                        pltpu.VMEM((nt, 1, tq), F32),
                        pltpu.SMEM((1,), I32),
                        pltpu.VMEM((tk, tq), F32),
                        pltpu.VMEM((tk, tq), BF16),
                        pltpu.VMEM((DSA_HEADS, LANES, tq), BF16),
                        pltpu.VMEM((IDX_HEADS, LANES, tq), BF16),
                        pltpu.VMEM((DSA_HEADS, 1, tq), F32),
                        pltpu.VMEM((DSA_HEADS, 1, tq), F32),
                        pltpu.VMEM((DSA_HEADS, HEAD_DIM, tq), F32),
                        pltpu.VMEM((DSA_HEADS, HEAD_DIM + BF16_ROWS, tq), F32)])
    return pl.pallas_call(
        functools.partial(_dsa_attn_kernel, k_sel),
        grid_spec=grid_spec,
        out_shape=jax.ShapeDtypeStruct((b, s, 1024), BF16),
        compiler_params=_params(("arbitrary", "arbitrary")),
        name="dsa_attn",
    )(it, jt, qt, k.reshape(b, s, 1024), vt, iqt, ik.reshape(b, s, LANES), iwt)


def _rope_tables(s):
    pos = jnp.arange(s, dtype=F32)
    inv_freq = ROPE_THETA ** (-jnp.arange(0, ROT_DIM, 2, dtype=F32) / ROT_DIM)
    ang = pos[:, None] * inv_freq[None, :]
    cos, sin = jnp.cos(ang), jnp.sin(ang)
    zeros = jnp.zeros((s, HEAD_DIM - ROT_DIM), F32)
    zero8 = jnp.zeros((s, ROT_HALF), F32)
    c = jnp.concatenate([cos, cos, zeros + 1.0], axis=1)
    s1 = jnp.concatenate([zero8, sin, zeros], axis=1)
    s2 = jnp.concatenate([-sin, zero8, zeros], axis=1)
    two = lambda a: jnp.concatenate([a, a], axis=1)
    return two(c), two(s1), two(s2), cos.T, sin.T


def kernel(x, norm_mix, w_in_even, b_forget, lambda_q1, lambda_k1, lambda_q2, lambda_k2,
           diff_subln_g, w_out_even, w_in_odd, idx_ln_g, idx_ln_b, w_out_odd, norm_mlp,
           w_mlp_in, w_mlp_out, norm_final):
    b, s, d = x.shape
    tabs = _rope_tables(s)
    h = x.reshape(b * s, d)

    lambda_init = 0.8 - 0.6 * math.exp(-0.3 * 0)
    ak, fk, fl, aqt, avt, fqt, fvt = _even_proj(h, norm_mix[0], w_in_even[0], tabs, b, s)
    c, ct = _fox_cumsum(fl, b_forget[0], b, s)
    oa, ob = _l0_attn(aqt, ak, avt, fqt, fk, fvt, c, ct, lambda_q1[0], lambda_k1[0],
                      lambda_q2[0], lambda_k2[0], diff_subln_g[0], lambda_init, b, s)
    h = _outproj_mlp(h, [oa.reshape(b * s, 512), ob.reshape(b * s, 512)],
                     [w_out_even[0][0:512], w_out_even[0][512:1024]], norm_mlp[0],
                     w_mlp_in[0], w_mlp_out[0], norm_final, False, s)

    k, ik, qt, vt, iqt, iwt = _odd_proj(h, norm_mix[1], w_in_odd[0], idx_ln_g[0], idx_ln_b[0],
                                        tabs, b, s)
    o = _dsa_attn(qt, k, vt, iqt, ik, iwt, b, s)
    h = _outproj_mlp(h, [o.reshape(b * s, 1024)], [w_out_odd[0]], norm_mlp[1],
                     w_mlp_in[1], w_mlp_out[1], norm_final, True, s)
    return h.reshape(b, s, d)
```

```python
import functools
import math
from typing import Any, NamedTuple

import jax
import jax.numpy as jnp
from jax import lax
from jax.experimental import pallas as pl
from jax.experimental.pallas import tpu as pltpu

F32 = jnp.float32
BF16 = jnp.bfloat16
I32 = jnp.int32

HEAD_DIM = 64
ROT_DIM = HEAD_DIM // 4
ROT_HALF = ROT_DIM // 2
ROPE_THETA = 500000.0
EPS = 1e-6
DIFF_HEADS = 4
FOX_HEADS = 8
DSA_HEADS = 16
IDX_HEADS = 8
TOPK_MAX = 256

LANES = 128
SUBLANES = 8
BF16_ROWS = 16
NEG = -1e30
MAX_SLACK = 16.0
INT_MIN = -(2 ** 31)
NEG_INF_ORDINAL = INT_MIN + 0x7FFFFF
LOG2E = math.log2(math.e)
Q_SCALE = HEAD_DIM ** -0.5 * LOG2E
VMEM_LIMIT = 56 * 1024 * 1024

ROW_TILE = 512
FF_TILE = 1024
CUMSUM_TILE = 256
L0_TILE = 512
DSA_TILE = 512
COUNT_ROWS = 128


def _params(sem):
    return pltpu.CompilerParams(dimension_semantics=sem, vmem_limit_bytes=VMEM_LIMIT)


def _rms(x, g):
    return x * lax.rsqrt(jnp.mean(x * x, axis=1, keepdims=True) + EPS) * g


def _dot(a, b):
    return jnp.dot(a, b, preferred_element_type=F32)


def _dot_nt(a, b):
    return lax.dot_general(a, b, (((1,), (1,)), ((), ())), preferred_element_type=F32)


def _rope_lanes(x, c, s1, s2):
    return x * c + pltpu.roll(x, ROT_HALF, 1) * s1 + pltpu.roll(x, LANES - ROT_HALF, 1) * s2


def _rope_rows(xh, cos_t, sin_t):
    x1 = xh[0:ROT_HALF]
    x2 = xh[ROT_HALF:ROT_DIM]
    return jnp.concatenate(
        [x1 * cos_t - x2 * sin_t, x2 * cos_t + x1 * sin_t, xh[ROT_DIM:]], axis=0)


def _even_proj_kernel(x_ref, g_ref, w_ref, wt_ref, c_ref, s1_ref, s2_ref, cos_ref, sin_ref,
                      ak_ref, fk_ref, fl_ref, aqt_ref, avt_ref, fqt_ref, fvt_ref):
    hn = _rms(x_ref[...], g_ref[...]).astype(BF16)
    c, s1, s2 = c_ref[...], s1_ref[...], s2_ref[...]
    ak = _dot(hn, w_ref[:, 0:512])
    for grp in range(DIFF_HEADS):
        lo = grp * LANES
        ak_ref[:, lo:lo + LANES] = _rope_lanes(ak[:, lo:lo + LANES], c, s1, s2).astype(BF16)
    fk_ref[...] = _dot(hn, w_ref[:, 512:1024]).astype(BF16)
    fl_ref[...] = _dot(hn, w_ref[:, 1024:1024 + LANES])
    cos_t, sin_t = cos_ref[...], sin_ref[...]
    aqt = _dot_nt(wt_ref[0:512, :], hn)
    for hd in range(2 * DIFF_HEADS):
        lo = hd * HEAD_DIM
        qt = _rope_rows(aqt[lo:lo + HEAD_DIM], cos_t, sin_t) * Q_SCALE
        aqt_ref[0, lo:lo + HEAD_DIM, :] = qt.astype(BF16)
    avt_ref[0] = _dot_nt(wt_ref[512:1024, :], hn).astype(BF16)
    fqt_ref[0] = (_dot_nt(wt_ref[1024:1536, :], hn) * Q_SCALE).astype(BF16)
    fvt_ref[0] = _dot_nt(wt_ref[1536:2048, :], hn).astype(BF16)


def _even_proj(h2d, g, w_in, tabs, b, s):
    n, d = h2d.shape
    tm = min(ROW_TILE, s)
    nb = s // tm
    a_q, a_k, a_v, f_q, f_k, f_v, f_l = (
        w_in[:, 0:512], w_in[:, 512:1024], w_in[:, 1024:1536], w_in[:, 1536:2048],
        w_in[:, 2048:2560], w_in[:, 2560:3072], w_in[:, 3072:3080])
    f_l = jnp.pad(f_l, ((0, 0), (0, LANES - FOX_HEADS)))
    w = jnp.concatenate([a_k, f_k, f_l], axis=1).astype(BF16)
    wt = jnp.concatenate([a_q, a_v, f_q, f_v], axis=1).T.astype(BF16)
    c, s1, s2, cos_t, sin_t = tabs
    row = lambda i: (i, 0)
    pos = lambda i: (i % nb, 0)
    post = lambda i: (0, i % nb)
    fmaj = lambda i: (i // nb, 0, i % nb)
    const = lambda i: (0, 0)
    fm_shape = jax.ShapeDtypeStruct((b, 512, s), BF16)
    fm_spec = pl.BlockSpec((1, 512, tm), fmaj)
    return pl.pallas_call(
        _even_proj_kernel,
        grid=(n // tm,),
        in_specs=[
            pl.BlockSpec((tm, d), row), pl.BlockSpec((1, d), const),
            pl.BlockSpec(w.shape, const), pl.BlockSpec(wt.shape, const),
            pl.BlockSpec((tm, LANES), pos), pl.BlockSpec((tm, LANES), pos),
            pl.BlockSpec((tm, LANES), pos),
            pl.BlockSpec((ROT_HALF, tm), post), pl.BlockSpec((ROT_HALF, tm), post)],
        out_specs=(
            pl.BlockSpec((tm, 512), row), pl.BlockSpec((tm, 512), row),
            pl.BlockSpec((tm, LANES), row), fm_spec, fm_spec, fm_spec, fm_spec),
        out_shape=(
            jax.ShapeDtypeStruct((n, 512), BF16), jax.ShapeDtypeStruct((n, 512), BF16),
            jax.ShapeDtypeStruct((n, LANES), F32), fm_shape, fm_shape, fm_shape, fm_shape),
        compiler_params=_params(("arbitrary",)),
        name="even_proj",
    )(h2d, g.reshape(1, d), w, wt, c, s1, s2, cos_t, sin_t)


def _fox_cumsum_kernel(fl_ref, bf_ref, c_ref, ct_ref, carry_ref):
    @pl.when(pl.program_id(1) == 0)
    def _():
        carry_ref[...] = jnp.zeros_like(carry_ref)

    z = fl_ref[0] + bf_ref[...]
    logf = jnp.minimum(z, 0.0) - jnp.log1p(jnp.exp(-jnp.abs(z)))
    t = z.shape[0]
    tri = (lax.broadcasted_iota(I32, (t, t), 0) >= lax.broadcasted_iota(I32, (t, t), 1)).astype(F32)
    cs = jnp.dot(tri, logf, precision=lax.Precision.HIGHEST,
                 preferred_element_type=F32) + carry_ref[...]
    c_ref[0] = cs
    ct_ref[0] = cs.T[0:FOX_HEADS, :]
    carry_ref[...] = cs[t - 1:t, :]


def _fox_cumsum(fl, b_f, b, s):
    tc = min(CUMSUM_TILE, s)
    bf = jnp.pad(b_f.astype(F32), (0, LANES - FOX_HEADS)).reshape(1, LANES)
    return pl.pallas_call(
        _fox_cumsum_kernel,
        grid=(b, s // tc),
        in_specs=[pl.BlockSpec((1, tc, LANES), lambda bi, i: (bi, i, 0)),
                  pl.BlockSpec((1, LANES), lambda bi, i: (0, 0))],
        out_specs=(pl.BlockSpec((1, tc, LANES), lambda bi, i: (bi, i, 0)),
                   pl.BlockSpec((1, FOX_HEADS, tc), lambda bi, i: (bi, 0, i))),
        out_shape=(jax.ShapeDtypeStruct((b, s, LANES), F32),
                   jax.ShapeDtypeStruct((b, FOX_HEADS, s), F32)),
        scratch_shapes=[pltpu.VMEM((1, LANES), F32)],
        compiler_params=_params(("arbitrary", "arbitrary")),
        name="fox_cumsum",
    )(fl.reshape(b, s, LANES), bf)


def _fill_masked_queries(qt_ref, qz_ref, n_heads):
    row = lax.broadcasted_iota(I32, (LANES, qt_ref.shape[2]), 0)
    for hd in range(n_heads):
        grp = hd // 2
        qg = qt_ref[0, grp * LANES:(grp + 1) * LANES, :]
        keep = (row < HEAD_DIM) if hd % 2 == 0 else (row >= HEAD_DIM)
        qz_ref[hd] = jnp.where(keep, qg, jnp.zeros_like(qg))


def _init_state(m_ref, l_ref, acc_ref):
    m_ref[...] = jnp.full(m_ref.shape, NEG, F32)
    l_ref[...] = jnp.zeros_like(l_ref)
    acc_ref[...] = jnp.zeros_like(acc_ref)


def _fold_rows(x, op, group=SUBLANES):
    r, t = x.shape
    x = x.reshape(r // group, group, t)
    n = r // group
    while n > 1:
        assert n % 2 == 0
        n //= 2
        x = op(x[:n], x[n:])
    return x[0]


def _reduce_rows(x, op):
    x = _fold_rows(x, op)
    n = SUBLANES
    while n > 1:
        n //= 2
        x = op(x[:n], x[n:])
    return x


class _HeadState(NamedTuple):
    m: Any
    l: Any
    acc: Any
    pv: Any


def _head_states(m_ref, l_ref, acc_ref, pv_ref):
    return [_HeadState(m_ref.at[h], l_ref.at[h], acc_ref.at[h], pv_ref.at[h])
            for h in range(m_ref.shape[0])]


def _softmax_update(st, head):
    m_prev = head.m[...]
    m_next = jnp.maximum(m_prev, _reduce_rows(st, jnp.maximum))
    alpha = jnp.exp2(m_prev - m_next)
    p = jnp.exp2(st - m_next)
    head.l[...] = alpha * head.l[...] + _reduce_rows(p, jnp.add)
    head.m[...] = m_next
    return alpha, p.astype(BF16)


def _attend_heads(heads, logits_fn, value_fn):
    st = logits_fn(0)
    for hd, head in enumerate(heads):
        st_next = logits_fn(hd + 1) if hd + 1 < len(heads) else None
        alpha, p = _softmax_update(st, head)
        head.acc[...] = alpha * head.acc[...] + _dot(value_fn(hd), p)
        st = st_next


def _weight_rows(w_row):
    rows = lax.broadcasted_iota(I32, (BF16_ROWS, w_row.shape[1]), 0)
    return jnp.where(rows == 0, w_row, 0.0).astype(BF16)


def _attend_heads_lazy(heads, raw_logits_fn, weighted_value_fn, keep, logits_fn, value_fn):
    excess = jnp.full(heads[0].m.shape, NEG, F32)
    st = raw_logits_fn(0)
    for hd, head in enumerate(heads):
        st_next = raw_logits_fn(hd + 1) if hd + 1 < len(heads) else None
        m_prev = head.m[...]
        excess = jnp.maximum(excess, _reduce_rows(st, jnp.maximum) - m_prev)
        p = jnp.exp2(st - m_prev).astype(BF16)
        head.pv[...] = _dot(weighted_value_fn(hd), p if keep is None else p * keep)
        st = st_next
    within = jnp.max(excess) <= MAX_SLACK

    @pl.when(within)
    def _():
        for head in heads:
            d = head.acc.shape[0]
            head.acc[...] = head.acc[...] + head.pv[0:d]
            head.l[...] = head.l[...] + head.pv[d:d + 1]

    @pl.when(jnp.logical_not(within))
    def _():
        _attend_heads(heads, logits_fn, value_fn)


def _causal_pairs(nt):
    it = [i for i in range(nt) for _ in range(i + 1)]
    jt = [j for i in range(nt) for j in range(i, -1, -1)]
    return jnp.asarray(it, I32), jnp.asarray(jt, I32)


_TOK_Q = lambda bi, p, it, jt: (bi, it[p], 0)
_FM_Q = lambda bi, p, it, jt: (bi, 0, it[p])
_TOK_K = lambda bi, p, it, jt: (bi, jt[p], 0)
_FM_K = lambda bi, p, it, jt: (bi, 0, jt[p])


def _causal_tile(tk, tq, k0, q0):
    kpos = k0 + lax.broadcasted_iota(I32, (tk, tq), 0)
    qpos = q0 + lax.broadcasted_iota(I32, (tk, tq), 1)
    return kpos <= qpos


def _l0_attn_kernel(lambda_init, it_ref, jt_ref, aqt_ref, ak_ref, avt_ref, fqt_ref, fk_ref,
                    fvt_ref, ck_ref, cq_ref, ctk_ref, ctq_ref, lq1_ref, lk1_ref, lq2_ref,
                    lk2_ref, subg_ref, oa_ref, ob_ref, aqz_ref, am_ref, al_ref, aacc_ref,
                    apv_ref, fqz_ref, fm_ref, fl_ref, facc_ref, fpv_ref):
    i, j = it_ref[pl.program_id(1)], jt_ref[pl.program_id(1)]
    tq, tk = aqt_ref.shape[2], ak_ref.shape[1]
    n_maps = 2 * DIFF_HEADS
    heads = (_head_states(am_ref, al_ref, aacc_ref, apv_ref)
             + _head_states(fm_ref, fl_ref, facc_ref, fpv_ref))

    def raw_logits(mp):
        if mp < n_maps:
            return _dot(ak_ref[0, :, (mp // 2) * LANES:(mp // 2 + 1) * LANES], aqz_ref[mp])
        hd = mp - n_maps
        return _dot(fk_ref[0, :, (hd // 2) * LANES:(hd // 2 + 1) * LANES], fqz_ref[hd])

    def values(mp):
        if mp < n_maps:
            return avt_ref[0, (mp // 2) * LANES:(mp // 2 + 1) * LANES, :]
        hd = mp - n_maps
        return fvt_ref[0, hd * HEAD_DIM:(hd + 1) * HEAD_DIM, :]

    def make_logits(causal):
        decay = (cq_ref[0, 0:1, :] - ck_ref[0]) * LOG2E

        def logits(mp):
            st = raw_logits(mp)
            if mp >= n_maps:
                st = st + decay[:, mp - n_maps:mp - n_maps + 1]
            return st if causal is None else jnp.where(causal, st, NEG)
        return logits

    @pl.when(j == i)
    def _():
        _fill_masked_queries(aqt_ref, aqz_ref, n_maps)
        _fill_masked_queries(fqt_ref, fqz_ref, FOX_HEADS)
        _init_state(am_ref, al_ref, aacc_ref)
        _init_state(fm_ref, fl_ref, facc_ref)
        _attend_heads(heads, make_logits(_causal_tile(tk, tq, 0, 0)), values)

    @pl.when(j < i)
    def _():
        w = jnp.exp2((ctq_ref[0, :, 0:1] - ctk_ref[0]) * LOG2E)
        ones = _weight_rows(jnp.ones((1, tk), F32))

        def weighted_values(mp):
            if mp < n_maps:
                return jnp.concatenate([values(mp), ones], axis=0)
            w_row = w[mp - n_maps:mp - n_maps + 1]
            return jnp.concatenate([values(mp) * w_row.astype(BF16), _weight_rows(w_row)], axis=0)

        _attend_heads_lazy(heads, raw_logits, weighted_values, None, make_logits(None), values)

    @pl.when(j == 0)
    def _():
        lam = (jnp.exp(jnp.sum(lq1_ref[...] * lk1_ref[...], axis=1, keepdims=True))
               - jnp.exp(jnp.sum(lq2_ref[...] * lk2_ref[...], axis=1, keepdims=True))
               + lambda_init)
        for hd in range(DIFF_HEADS):
            o = (aacc_ref[2 * hd] / al_ref[2 * hd]
                 - lam * (aacc_ref[2 * hd + 1] / al_ref[2 * hd + 1]))
            o = o * lax.rsqrt(jnp.mean(o * o, axis=0, keepdims=True) + EPS)
            o = o.T * (subg_ref[...] * (1.0 - lambda_init))
            oa_ref[0, :, hd * LANES:(hd + 1) * LANES] = o.astype(BF16)
        for grp in range(FOX_HEADS // 2):
            o = jnp.concatenate([facc_ref[2 * grp] / fl_ref[2 * grp],
                                 facc_ref[2 * grp + 1] / fl_ref[2 * grp + 1]], axis=0)
            ob_ref[0, :, grp * LANES:(grp + 1) * LANES] = o.T.astype(BF16)


def _l0_attn(aqt, ak, avt, fqt, fk, fvt, c, ct, lq1, lk1, lq2, lk2, sub_g, lambda_init, b, s):
    t = min(L0_TILE, s)
    it, jt = _causal_pairs(s // t)
    const = lambda bi, p, it, jt: (0, 0)
    first_rows = lambda bi, p, it, jt: (bi, it[p] * (t // SUBLANES), 0)
    vec = lambda a: a.astype(F32).reshape(1, -1)
    n_maps = 2 * DIFF_HEADS
    qkv_specs = [pl.BlockSpec((1, 512, t), _FM_Q), pl.BlockSpec((1, t, 512), _TOK_K),
                 pl.BlockSpec((1, 512, t), _FM_K)]
    grid_spec = pltpu.PrefetchScalarGridSpec(
        num_scalar_prefetch=2,
        grid=(b, it.shape[0]),
        in_specs=qkv_specs + qkv_specs + [
            pl.BlockSpec((1, t, LANES), _TOK_K), pl.BlockSpec((1, SUBLANES, LANES), first_rows),
            pl.BlockSpec((1, FOX_HEADS, t), _FM_K), pl.BlockSpec((1, FOX_HEADS, t), _FM_Q),
            pl.BlockSpec((1, HEAD_DIM), const), pl.BlockSpec((1, HEAD_DIM), const),
            pl.BlockSpec((1, HEAD_DIM), const), pl.BlockSpec((1, HEAD_DIM), const),
            pl.BlockSpec((1, LANES), const)],
        out_specs=(pl.BlockSpec((1, t, 512), _TOK_Q), pl.BlockSpec((1, t, 512), _TOK_Q)),
        scratch_shapes=[pltpu.VMEM((n_maps, LANES, t), BF16),
                        pltpu.VMEM((n_maps, 1, t), F32),
                        pltpu.VMEM((n_maps, 1, t), F32),
                        pltpu.VMEM((n_maps, LANES, t), F32),
                        pltpu.VMEM((n_maps, LANES + BF16_ROWS, t), F32),
                        pltpu.VMEM((FOX_HEADS, LANES, t), BF16),
                        pltpu.VMEM((FOX_HEADS, 1, t), F32),
                        pltpu.VMEM((FOX_HEADS, 1, t), F32),
                        pltpu.VMEM((FOX_HEADS, HEAD_DIM, t), F32),
                        pltpu.VMEM((FOX_HEADS, HEAD_DIM + BF16_ROWS, t), F32)])
    return pl.pallas_call(
        functools.partial(_l0_attn_kernel, lambda_init),
        grid_spec=grid_spec,
        out_shape=(jax.ShapeDtypeStruct((b, s, 512), BF16),
                   jax.ShapeDtypeStruct((b, s, 512), BF16)),
        compiler_params=_params(("arbitrary", "arbitrary")),
        name="l0_attn",
    )(it, jt, aqt, ak.reshape(b, s, 512), avt, fqt, fk.reshape(b, s, 512), fvt, c, c, ct, ct,
      vec(lq1), vec(lk1), vec(lq2), vec(lk2), vec(sub_g))


def _mlp_kernel(n_mix, final, *refs):
    h_ref = refs[0]
    mix_refs = refs[1:1 + n_mix]
    wo_refs = refs[1 + n_mix:1 + 2 * n_mix]
    g_ref, w1_ref, w2_ref, gf_ref, out_ref, h1_ref, hn_ref, acc_ref = refs[1 + 2 * n_mix:]
    f = pl.program_id(1)

    @pl.when(f == 0)
    def _():
        h1 = h_ref[...]
        for mix_ref, wo_ref in zip(mix_refs, wo_refs):
            h1 = h1 + _dot(mix_ref[...], wo_ref[...])
        h1_ref[...] = h1
        hn_ref[...] = _rms(h1, g_ref[...]).astype(BF16)
        acc_ref[...] = jnp.zeros_like(acc_ref)

    u = _dot(hn_ref[...], w1_ref[...])
    a = jnp.square(jnp.maximum(u, 0.0)).astype(BF16)
    acc_ref[...] += _dot(a, w2_ref[...])

    @pl.when(f == pl.num_programs(1) - 1)
    def _():
        y = h1_ref[...] + acc_ref[...]
        if final:
            y = _rms(y, gf_ref[...])
        out_ref[...] = y


def _outproj_mlp(h2d, mixes, w_outs, g, w1, w2, g_final, final, s):
    n, d = h2d.shape
    dff = w1.shape[1]
    tm = min(ROW_TILE, s)
    tf = min(FF_TILE, dff)
    row = lambda i, f: (i, 0)
    const = lambda i, f: (0, 0)
    in_specs = [pl.BlockSpec((tm, d), row)]
    in_specs += [pl.BlockSpec((tm, m.shape[1]), row) for m in mixes]
    in_specs += [pl.BlockSpec(w.shape, const) for w in w_outs]
    in_specs += [pl.BlockSpec((1, d), const),
                 pl.BlockSpec((d, tf), lambda i, f: (0, f)),
                 pl.BlockSpec((tf, d), lambda i, f: (f, 0)),
                 pl.BlockSpec((1, d), const)]
    return pl.pallas_call(
        functools.partial(_mlp_kernel, len(mixes), final),
        grid=(n // tm, dff // tf),
        in_specs=in_specs,
        out_specs=pl.BlockSpec((tm, d), row),
        out_shape=jax.ShapeDtypeStruct((n, d), F32),
        scratch_shapes=[pltpu.VMEM((tm, d), F32), pltpu.VMEM((tm, d), BF16),
                        pltpu.VMEM((tm, d), F32)],
        compiler_params=_params(("arbitrary", "arbitrary")),
        name="outproj_mlp",
    )(h2d, *mixes, *[w.astype(BF16) for w in w_outs], g.reshape(1, d), w1.astype(BF16),
      w2.astype(BF16), g_final.reshape(1, d))


def _odd_proj_kernel(x_ref, g_ref, w_ref, wt_ref, lng_ref, lnb_ref, c_ref, s1_ref, s2_ref,
                     cos_ref, sin_ref, k_ref, ik_ref, qt_ref, vt_ref, iqt_ref, iwt_ref):
    hn = _rms(x_ref[...], g_ref[...]).astype(BF16)
    c, s1, s2 = c_ref[...], s1_ref[...], s2_ref[...]
    k = _dot(hn, w_ref[:, 0:1024])
    for grp in range(DSA_HEADS // 2):
        lo = grp * LANES
        k_ref[:, lo:lo + LANES] = _rope_lanes(k[:, lo:lo + LANES], c, s1, s2).astype(BF16)
    ik = _dot(hn, w_ref[:, 1024:1024 + LANES])
    mu = jnp.mean(ik, axis=1, keepdims=True)
    var = jnp.mean(jnp.square(ik - mu), axis=1, keepdims=True)
    ik = (ik - mu) * lax.rsqrt(var + EPS) * lng_ref[...] + lnb_ref[...]
    ik_ref[...] = _rope_lanes(ik, c, s1, s2).astype(BF16)
    cos_t, sin_t = cos_ref[...], sin_ref[...]
    qt_all = _dot_nt(wt_ref[0:1024, :], hn)
    for hd in range(DSA_HEADS):
        lo = hd * HEAD_DIM
        qt = _rope_rows(qt_all[lo:lo + HEAD_DIM], cos_t, sin_t) * Q_SCALE
        qt_ref[0, lo:lo + HEAD_DIM, :] = qt.astype(BF16)
    vt_ref[0] = _dot_nt(wt_ref[1024:2048, :], hn).astype(BF16)
    iqt_all = _dot_nt(wt_ref[2048:2560 + BF16_ROWS, :], hn)
    for hd in range(IDX_HEADS):
        lo = hd * HEAD_DIM
        iqt = _rope_rows(iqt_all[lo:lo + HEAD_DIM], cos_t, sin_t) * HEAD_DIM ** -0.5
        iqt_ref[0, lo:lo + HEAD_DIM, :] = iqt.astype(BF16)
    iwt_ref[0] = iqt_all[512:512 + IDX_HEADS] * IDX_HEADS ** -0.5


def _odd_proj(h2d, g, w_in, ln_g, ln_b, tabs, b, s):
    n, d = h2d.shape
    tm = min(ROW_TILE, s)
    nb = s // tm
    wq, wk, wv, wiq, wik, wiw = (
        w_in[:, 0:1024], w_in[:, 1024:2048], w_in[:, 2048:3072], w_in[:, 3072:3584],
        w_in[:, 3584:3648], w_in[:, 3648:3656])
    wiw = jnp.pad(wiw, ((0, 0), (0, BF16_ROWS - IDX_HEADS)))
    w = jnp.concatenate([wk, wik, wik], axis=1).astype(BF16)
    wt = jnp.concatenate([wq, wv, wiq, wiw], axis=1).T.astype(BF16)
    two = lambda a: jnp.concatenate([a, a]).astype(F32).reshape(1, LANES)
    c, s1, s2, cos_t, sin_t = tabs
    row = lambda i: (i, 0)
    pos = lambda i: (i % nb, 0)
    post = lambda i: (0, i % nb)
    fmaj = lambda i: (i // nb, 0, i % nb)
    const = lambda i: (0, 0)
    return pl.pallas_call(
        _odd_proj_kernel,
        grid=(n // tm,),
        in_specs=[
            pl.BlockSpec((tm, d), row), pl.BlockSpec((1, d), const),
            pl.BlockSpec(w.shape, const), pl.BlockSpec(wt.shape, const),
            pl.BlockSpec((1, LANES), const), pl.BlockSpec((1, LANES), const),
            pl.BlockSpec((tm, LANES), pos), pl.BlockSpec((tm, LANES), pos),
            pl.BlockSpec((tm, LANES), pos),
            pl.BlockSpec((ROT_HALF, tm), post), pl.BlockSpec((ROT_HALF, tm), post)],
        out_specs=(
            pl.BlockSpec((tm, 1024), row), pl.BlockSpec((tm, LANES), row),
            pl.BlockSpec((1, 1024, tm), fmaj), pl.BlockSpec((1, 1024, tm), fmaj),
            pl.BlockSpec((1, 512, tm), fmaj), pl.BlockSpec((1, IDX_HEADS, tm), fmaj)),
        out_shape=(
            jax.ShapeDtypeStruct((n, 1024), BF16), jax.ShapeDtypeStruct((n, LANES), BF16),
            jax.ShapeDtypeStruct((b, 1024, s), BF16), jax.ShapeDtypeStruct((b, 1024, s), BF16),
            jax.ShapeDtypeStruct((b, 512, s), BF16), jax.ShapeDtypeStruct((b, IDX_HEADS, s), F32)),
        compiler_params=_params(("arbitrary",)),
        name="odd_proj",
    )(h2d, g.reshape(1, d), w, wt, two(ln_g), two(ln_b), c, s1, s2, cos_t, sin_t)


def _dsa_attn_kernel(k_sel, it_ref, jt_ref, qt_ref, k_ref, vt_ref, iqt_ref, ik_ref, iwt_ref,
                     o_ref, score_ref, coarse_ref, thr_ref, need_ref, tie_ref, flag_ref, bias_ref,
                     keep_ref, qz_ref, iqz_ref, m_ref, l_ref, acc_ref, pv_ref):
    i, j = it_ref[pl.program_id(1)], jt_ref[pl.program_id(1)]
    tq, tk = qt_ref.shape[2], k_ref.shape[1]

    def count_ge(cand):
        def body(c, part):
            return part + _fold_rows(jnp.where(score_ref[c] >= cand, 1, 0), jnp.add)
        part = lax.fori_loop(0, i + 1, body, jnp.zeros((SUBLANES, tq), I32))
        return jnp.sum(part, axis=0, keepdims=True)

    def count_ge_coarse(cand):
        def body(c, part):
            for r0 in range(0, tk, COUNT_ROWS):
                hit = jnp.where(coarse_ref[c, r0:r0 + COUNT_ROWS, :] >= cand,
                                jnp.ones((), BF16), jnp.zeros((), BF16))
                part = part + _fold_rows(hit, jnp.add, BF16_ROWS).astype(F32)
            return part
        part = lax.fori_loop(0, i + 1, body, jnp.zeros((BF16_ROWS, tq), F32))
        return jnp.sum(part, axis=0, keepdims=True).astype(I32)

    def ordinal_to_float(o):
        x = lax.bitcast_convert_type(o ^ ((o >> 31) & 0x7FFFFFFF), F32)
        return jnp.where(o < NEG_INF_ORDINAL, -jnp.inf, x)

    @pl.when(j == i)
    def _():
        _fill_masked_queries(qt_ref, qz_ref, DSA_HEADS)
        _fill_masked_queries(iqt_ref, iqz_ref, IDX_HEADS)
        _init_state(m_ref, l_ref, acc_ref)
        iw = iwt_ref[0]

        def score_tile(c, carry):
            ik = ik_ref[0, pl.ds(pl.multiple_of(c * tk, tk), tk), :]
            score = jnp.zeros((tk, tq), F32)
            for hd in range(IDX_HEADS):
                score = score + jnp.maximum(_dot(ik, iqz_ref[hd]), 0.0) * iw[hd:hd + 1, :]
            score = jnp.where(score == 0.0, 0.0, score)
            score = jnp.where(_causal_tile(tk, tq, c * tk, i * tq), score, -jnp.inf)
            score_ref[c] = score
            coarse_ref[c] = score.astype(BF16)
            return carry
        lax.fori_loop(0, i + 1, score_tile, 0)

        def coarse_step(step, o):
            cand = o + lax.shift_left(jnp.int32(1), 15 - step)
            cand_f = ordinal_to_float(lax.shift_left(cand, 16)).astype(BF16)
            return jnp.where(count_ge_coarse(cand_f) >= k_sel, cand, o)
        coarse = lax.fori_loop(0, 16, coarse_step, jnp.full((1, tq), -(2 ** 15), I32))
        few = coarse <= (NEG_INF_ORDINAL >> 16)
        base = lax.shift_left(jnp.where(few, 0, coarse) - 1, 16)

        def fine_step(step, u):
            cand = u + lax.shift_left(jnp.int32(1), 16 - step)
            return jnp.where(count_ge(ordinal_to_float(base + cand)) >= k_sel, cand, u)
        fine = lax.fori_loop(0, 17, fine_step, jnp.zeros((1, tq), I32))
        o = jnp.where(few, NEG_INF_ORDINAL, base + fine)
        thr = ordinal_to_float(o)
        n_ge = count_ge(thr)
        real = thr > -jnp.inf
        thr_ref[...] = thr
        flag_ref[0] = jnp.max(jnp.where(real & (n_ge > k_sel), 1, 0))

        @pl.when(flag_ref[0] != 0)
        def _():
            n_gt = count_ge(ordinal_to_float(o + 1))
            need_ref[...] = jnp.where(real, k_sel - n_gt, 0).astype(F32)

            def tie_tile(c, seen):
                tie_ref[c] = seen
                hit = (score_ref[c] == thr) & real
                return seen + jnp.sum(jnp.where(hit, 1.0, 0.0), axis=0, keepdims=True)
            lax.fori_loop(0, i + 1, tie_tile, jnp.zeros((1, tq), F32))

    thr = thr_ref[...]
    kc = score_ref[j]

    def store_selection(sel):
        bias_ref[...] = jnp.where(sel, 0.0, NEG)
        keep_ref[...] = jnp.where(sel, 1.0, 0.0).astype(BF16)

    @pl.when(flag_ref[0] == 0)
    def _():
        store_selection(kc >= jnp.maximum(thr, jnp.finfo(F32).min))

    @pl.when(flag_ref[0] != 0)
    def _():
        gt = kc > thr
        eq = (kc == thr) & (thr > -jnp.inf)
        earlier = (lax.broadcasted_iota(I32, (tk, tk), 1) < lax.broadcasted_iota(I32, (tk, tk), 0))
        eq_f = jnp.where(eq, 1.0, 0.0).astype(BF16)
        rank = _dot(jnp.where(earlier, 1.0, 0.0).astype(BF16), eq_f) + tie_ref[j]
        store_selection(gt | (eq & (rank < need_ref[...])))

    def raw_logits(hd):
        return _dot(k_ref[0, :, (hd // 2) * LANES:(hd // 2 + 1) * LANES], qz_ref[hd])

    def logits(hd):
        return raw_logits(hd) + bias_ref[...]

    def values(hd):
        return vt_ref[0, hd * HEAD_DIM:(hd + 1) * HEAD_DIM, :]

    heads = _head_states(m_ref, l_ref, acc_ref, pv_ref)

    @pl.when(j == i)
    def _():
        _attend_heads(heads, logits, values)

    started = jnp.min(m_ref[0]) > 0.5 * NEG

    @pl.when((j < i) & started)
    def _():
        ones = _weight_rows(jnp.ones((1, tk), F32))

        def weighted_values(hd):
            return jnp.concatenate([values(hd), ones], axis=0)

        _attend_heads_lazy(heads, raw_logits, weighted_values, keep_ref[...], logits, values)

    @pl.when((j < i) & jnp.logical_not(started))
    def _():
        _attend_heads(heads, logits, values)

    @pl.when(j == 0)
    def _():
        for grp in range(DSA_HEADS // 2):
            o = jnp.concatenate([acc_ref[2 * grp] / l_ref[2 * grp],
                                 acc_ref[2 * grp + 1] / l_ref[2 * grp + 1]], axis=0)
            o_ref[0, :, grp * LANES:(grp + 1) * LANES] = o.T.astype(BF16)


def _dsa_attn(qt, k, vt, iqt, ik, iwt, b, s):
    t = min(DSA_TILE, s)
    nt = s // t
    it, jt = _causal_pairs(nt)
    k_sel = min(TOPK_MAX, s // 4)
    assert t >= k_sel
    tq = tk = t
    grid_spec = pltpu.PrefetchScalarGridSpec(
        num_scalar_prefetch=2,
        grid=(b, it.shape[0]),
        in_specs=[pl.BlockSpec((1, 1024, t), _FM_Q), pl.BlockSpec((1, t, 1024), _TOK_K),
                  pl.BlockSpec((1, 1024, t), _FM_K), pl.BlockSpec((1, 512, t), _FM_Q),
                  pl.BlockSpec((1, s, LANES), lambda bi, p, it, jt: (bi, 0, 0)),
                  pl.BlockSpec((1, IDX_HEADS, t), _FM_Q)],
        out_specs=pl.BlockSpec((1, t, 1024), _TOK_Q),
        scratch_shapes=[pltpu.VMEM((nt, tk, tq), F32),
                        pltpu.VMEM((nt, tk, tq), BF16),
                        pltpu.VMEM((1, tq), F32),
                        pltpu.VMEM((1, tq), F32),
                        pltpu.VMEM((nt, 1, tq), F32),
                        pltpu.SMEM((1,), I32),
                        pltpu.VMEM((tk, tq), F32),
                        pltpu.VMEM((tk, tq), BF16),
                        pltpu.VMEM((DSA_HEADS, LANES, tq), BF16),
                        pltpu.VMEM((IDX_HEADS, LANES, tq), BF16),
                        pltpu.VMEM((DSA_HEADS, 1, tq), F32),
                        pltpu.VMEM((DSA_HEADS, 1, tq), F32),
                        pltpu.VMEM((DSA_HEADS, HEAD_DIM, tq), F32),
                        pltpu.VMEM((DSA_HEADS, HEAD_DIM + BF16_ROWS, tq), F32)])
    return pl.pallas_call(
        functools.partial(_dsa_attn_kernel, k_sel),
        grid_spec=grid_spec,
        out_shape=jax.ShapeDtypeStruct((b, s, 1024), BF16),
        compiler_params=_params(("arbitrary", "arbitrary")),
        name="dsa_attn",
    )(it, jt, qt, k.reshape(b, s, 1024), vt, iqt, ik.reshape(b, s, LANES), iwt)


def _rope_tables(s):
    pos = jnp.arange(s, dtype=F32)
    inv_freq = ROPE_THETA ** (-jnp.arange(0, ROT_DIM, 2, dtype=F32) / ROT_DIM)
    ang = pos[:, None] * inv_freq[None, :]
    cos, sin = jnp.cos(ang), jnp.sin(ang)
    zeros = jnp.zeros((s, HEAD_DIM - ROT_DIM), F32)
    zero8 = jnp.zeros((s, ROT_HALF), F32)
    c = jnp.concatenate([cos, cos, zeros + 1.0], axis=1)
    s1 = jnp.concatenate([zero8, sin, zeros], axis=1)
    s2 = jnp.concatenate([-sin, zero8, zeros], axis=1)
    two = lambda a: jnp.concatenate([a, a], axis=1)
    return two(c), two(s1), two(s2), cos.T, sin.T


def kernel(x, norm_mix, w_in_even, b_forget, lambda_q1, lambda_k1, lambda_q2, lambda_k2,
           diff_subln_g, w_out_even, w_in_odd, idx_ln_g, idx_ln_b, w_out_odd, norm_mlp,
           w_mlp_in, w_mlp_out, norm_final):
    b, s, d = x.shape
    tabs = _rope_tables(s)
    h = x.reshape(b * s, d)

    lambda_init = 0.8 - 0.6 * math.exp(-0.3 * 0)
    ak, fk, fl, aqt, avt, fqt, fvt = _even_proj(h, norm_mix[0], w_in_even[0], tabs, b, s)
    c, ct = _fox_cumsum(fl, b_forget[0], b, s)
    oa, ob = _l0_attn(aqt, ak, avt, fqt, fk, fvt, c, ct, lambda_q1[0], lambda_k1[0],
                      lambda_q2[0], lambda_k2[0], diff_subln_g[0], lambda_init, b, s)
    h = _outproj_mlp(h, [oa.reshape(b * s, 512), ob.reshape(b * s, 512)],
                     [w_out_even[0][0:512], w_out_even[0][512:1024]], norm_mlp[0],
                     w_mlp_in[0], w_mlp_out[0], norm_final, False, s)

    k, ik, qt, vt, iqt, iwt = _odd_proj(h, norm_mix[1], w_in_odd[0], idx_ln_g[0], idx_ln_b[0],
                                        tabs, b, s)
    o = _dsa_attn(qt, k, vt, iqt, ik, iwt, b, s)
    h = _outproj_mlp(h, [o.reshape(b * s, 1024)], [w_out_odd[0]], norm_mlp[1],
                     w_mlp_in[1], w_mlp_out[1], norm_final, True, s)
    return h.reshape(b, s, d)
```
